```python
import jax, jax.numpy as jnp
from jax import lax
import numpy as np

D_MODEL = 1024
BATCH = 16
SEQ = 2048
DEPTH = 1

HEAD_DIM = 64
N_HEADS_DIL = 8
N_HEADS_SB = 8
DIL_WIDTH = N_HEADS_DIL * HEAD_DIM
SB_WIDTH = N_HEADS_SB * HEAD_DIM
DIL_PATTERNS = ((128, 1), (512, 4), (2048, 16))
BLOCK = 128
ROPE_THETA = 500000.0
ROPE_DIM = HEAD_DIM // 4
N_MEM = 256
N_HEADS_MEM = 4
MEM_HEAD_DIM = 128
MEM_WIDTH = N_HEADS_MEM * MEM_HEAD_DIM
D_FF = -(-(8 * D_MODEL) // (3 * 256)) * 256
IN_COLS = 3 * DIL_WIDTH + 3 * SB_WIDTH + 2 * D_MODEL
RMS_EPS = 1e-6
MAX_POS_OFFSET = 1024

kernel_name = "hybrid_dilated_stickbreak_gated_block"


def _rmsnorm(x, g):
    xf = x.astype(jnp.float32)
    y = xf * lax.rsqrt(jnp.mean(xf * xf, axis=-1, keepdims=True) + RMS_EPS)
    return (y * g.astype(jnp.float32)).astype(x.dtype)


def _partial_rope(x, positions):
    half = ROPE_DIM // 2
    inv_freq = ROPE_THETA ** (-jnp.arange(half, dtype=jnp.float32) / half)
    ang = positions.astype(jnp.float32)[:, None, :, None] * inv_freq
    cos, sin = jnp.cos(ang), jnp.sin(ang)
    xf = x.astype(jnp.float32)
    x1, x2 = xf[..., :half], xf[..., half:ROPE_DIM]
    out = jnp.concatenate([x1 * cos - x2 * sin, x2 * cos + x1 * sin, xf[..., ROPE_DIM:]], axis=-1)
    return out.astype(x.dtype)


def _dilated_pattern(q, k, v, window, dilation):
    B, H, S, hd = q.shape
    L = S // dilation
    n_back = window // dilation
    C = BLOCK
    nb = -(-L // C)
    Lp = nb * C

    def to_stream(t):
        t = t.reshape(B, H, L, dilation, hd).transpose(0, 1, 3, 2, 4)
        t = jnp.pad(t, ((0, 0), (0, 0), (0, 0), (0, Lp - L), (0, 0)))
        return t.reshape(B, H, dilation, nb, C, hd)

    def with_prev(t):
        prev = jnp.pad(t, ((0, 0), (0, 0), (0, 0), (1, 0), (0, 0), (0, 0)))[:, :, :, :-1]
        return jnp.concatenate([prev, t], axis=4)

    qs = to_stream(q)
    kb = with_prev(to_stream(k))
    vb = with_prev(to_stream(v)).astype(jnp.float32)
    s = jnp.einsum('bhrnqd,bhrnkd->bhrnqk', qs, kb,
                   preferred_element_type=jnp.float32) * (hd ** -0.5)
    i = jnp.arange(C)[:, None]
    j = jnp.arange(2 * C)[None, :]
    dist = C + i - j
    band = (dist >= 0) & (dist <= n_back)
    in_cur = j >= C
    blk = jnp.arange(nb)[:, None, None]
    valid = band[None] & ((blk > 0) | in_cur[None])
    s = jnp.where(valid, s, -jnp.inf)
    m = jnp.max(s, axis=-1, keepdims=True)
    p = jnp.exp(s - m)
    den = jnp.sum(p, axis=-1)
    o = jnp.einsum('bhrnqk,bhrnkd->bhrnqd', p, vb) / den[..., None]
    lse = m[..., 0] + jnp.log(den)

    def from_stream(t):
        rest = t.shape[5:]
        t = t.reshape((B, H, dilation, Lp) + rest)[:, :, :, :L]
        t = jnp.moveaxis(t, 2, 3)
        return t.reshape((B, H, S) + rest)

    return from_stream(o), from_stream(lse)


def _dilated_attention(q, k, v):
    outs, lses = [], []
    for window, dilation in DIL_PATTERNS:
        o, lse = _dilated_pattern(q, k, v, window, dilation)
        outs.append(o)
        lses.append(lse)
    alpha = jax.nn.softmax(jnp.stack(lses, axis=0), axis=0)
    return jnp.einsum('gbhs,gbhsd->bhsd', alpha, jnp.stack(outs, axis=0))


def _stick_breaking(q, k, v):
    B, H, S, hd = q.shape
    nb = S // BLOCK
    qb = q.reshape(B, H, nb, BLOCK, hd).transpose(2, 0, 1, 3, 4)
    vf = v.astype(jnp.float32)
    kpos = jnp.arange(S)

    def block(args):
        qi, bi = args
        z = jnp.einsum('bhqd,bhkd->bhqk', qi, k,
                       preferred_element_type=jnp.float32) * (hd ** -0.5)
        qpos = bi * BLOCK + jnp.arange(BLOCK)
        strict = kpos[None, :] < qpos[:, None]
        log1m = jnp.where(strict, jax.nn.log_sigmoid(-z), 0.0)
        between = lax.cumsum(log1m, axis=3, reverse=True) - log1m
        a = jnp.where(strict, jnp.exp(jax.nn.log_sigmoid(z) + between), 0.0)
        return jnp.einsum('bhqk,bhkd->bhqd', a, vf)

    o = lax.map(block, (qb, jnp.arange(nb)))
    return o.transpose(1, 2, 0, 3, 4).reshape(B, H, S, hd)


def _memory_attention(hn, mem_n, w_q, w_kv, w_o):
    B, S, _ = hn.shape
    q = (hn @ w_q).reshape(B, S, N_HEADS_MEM, MEM_HEAD_DIM)
    kv = mem_n @ w_kv
    k = kv[..., :MEM_WIDTH].reshape(B, N_MEM, N_HEADS_MEM, MEM_HEAD_DIM)
    v = kv[..., MEM_WIDTH:].reshape(B, N_MEM, N_HEADS_MEM, MEM_HEAD_DIM)
    s = jnp.einsum('bshd,bmhd->bhsm', q, k,
                   preferred_element_type=jnp.float32) * (MEM_HEAD_DIM ** -0.5)
    p = jax.nn.softmax(s, axis=-1)
    o = jnp.einsum('bhsm,bmhd->bshd', p, v.astype(jnp.float32))
    return o.reshape(B, S, MEM_WIDTH).astype(hn.dtype) @ w_o


def _swiglu(n, w_gate, w_up, w_down):
    return (jax.nn.silu(n @ w_gate) * (n @ w_up)) @ w_down


def setup_inputs(seed: int = 0) -> dict:
    key = jax.random.key(seed)
    ks = jax.random.split(key, 20)

    def w(k, shape):
        return jax.random.normal(k, shape, jnp.float32) * (shape[-2] ** -0.5)

    def gain(k, shape):
        return 1.0 + 0.01 * jax.random.normal(k, shape, jnp.float32)

    x = jax.random.normal(ks[0], (BATCH, SEQ, D_MODEL), jnp.float32)
    mem = jax.random.normal(ks[1], (BATCH, N_MEM, D_MODEL), jnp.float32)
    offs = jax.random.randint(ks[2], (BATCH, 1), 0, MAX_POS_OFFSET, dtype=jnp.int32)
    positions = offs + jnp.arange(SEQ, dtype=jnp.int32)[None, :]
    return {
        "x": x,
        "mem": mem,
        "positions": positions,
        "g_mix": gain(ks[3], (DEPTH, D_MODEL)),
        "w_in": w(ks[4], (DEPTH, D_MODEL, IN_COLS)),
        "w_up_a": w(ks[5], (DEPTH, DIL_WIDTH, D_MODEL)),
        "w_up_b": w(ks[6], (DEPTH, SB_WIDTH, D_MODEL)),
        "w_out": w(ks[7], (DEPTH, D_MODEL, D_MODEL)),
        "g_mem_q": gain(ks[8], (DEPTH, D_MODEL)),
        "g_mem_kv": gain(ks[9], (DEPTH, D_MODEL)),
        "w_q_mem": w(ks[10], (DEPTH, D_MODEL, MEM_WIDTH)),
        "w_kv_mem": w(ks[11], (DEPTH, D_MODEL, 2 * MEM_WIDTH)),
        "w_o_mem": w(ks[12], (DEPTH, MEM_WIDTH, D_MODEL)),
        "g_ffn": gain(ks[13], (DEPTH, D_MODEL)),
        "w_ffn_gate": w(ks[14], (DEPTH, D_MODEL, D_FF)),
        "w_ffn_up": w(ks[15], (DEPTH, D_MODEL, D_FF)),
        "w_ffn_down": w(ks[16], (DEPTH, D_FF, D_MODEL)),
        "g_final": gain(ks[17], (D_MODEL,)),
    }


def reference(x, mem, positions, g_mix, w_in, w_up_a, w_up_b, w_out, g_mem_q, g_mem_kv,
              w_q_mem, w_kv_mem, w_o_mem, g_ffn, w_ffn_gate, w_ffn_up, w_ffn_down, g_final):
    B, S, _ = x.shape
    split_at = list(np.cumsum([DIL_WIDTH, DIL_WIDTH, DIL_WIDTH,
                               SB_WIDTH, SB_WIDTH, SB_WIDTH, D_MODEL]))

    def heads(t, nh):
        return t.reshape(B, S, nh, HEAD_DIM).transpose(0, 2, 1, 3)

    def merge(t):
        return t.transpose(0, 2, 1, 3).reshape(B, S, -1).astype(x.dtype)

    h = x
    for l in range(DEPTH):
        n = _rmsnorm(h, g_mix[l])
        proj = n @ w_in[l]
        qa, ka, va, qb, kb, vb, gate_a, gate_b = jnp.split(proj, split_at, axis=-1)
        qa = _partial_rope(heads(qa, N_HEADS_DIL), positions)
        ka = _partial_rope(heads(ka, N_HEADS_DIL), positions)
        o_a = merge(_dilated_attention(qa, ka, heads(va, N_HEADS_DIL)))
        o_b = merge(_stick_breaking(heads(qb, N_HEADS_SB), heads(kb, N_HEADS_SB),
                                    heads(vb, N_HEADS_SB)))
        mixed = (jax.nn.sigmoid(gate_a) * (o_a @ w_up_a[l])
                 + jax.nn.sigmoid(gate_b) * (o_b @ w_up_b[l]))
        h = h + mixed @ w_out[l]
        h = h + _memory_attention(_rmsnorm(h, g_mem_q[l]), _rmsnorm(mem, g_mem_kv[l]),
                                  w_q_mem[l], w_kv_mem[l], w_o_mem[l])
        h = h + _swiglu(_rmsnorm(h, g_ffn[l]), w_ffn_gate[l], w_ffn_up[l], w_ffn_down[l])
    return _rmsnorm(h, g_final)
```

```python
import functools

import numpy as np
import jax
import jax.numpy as jnp
from jax import lax
from jax.experimental import pallas as pl
from jax.experimental.pallas import tpu as pltpu

F32 = jnp.float32
BF16 = jnp.bfloat16

HEAD_DIM = 64
MIX_WIDTH = 512
DIL_PATTERNS = ((128, 1), (512, 4), (2048, 16))
ROPE_THETA = 500000.0
ROPE_DIM = HEAD_DIM // 4
N_HEADS_MEM = 4
MEM_HEAD_DIM = 128
MEM_WIDTH = N_HEADS_MEM * MEM_HEAD_DIM
RMS_EPS = 1e-6

LANES = 128
MIB = 1024 * 1024
MASKED = -1e30

ATTN_BLOCK = 256
PROJ_ROWS = 256
MIX_ROWS = 256
FFN_ROWS = 512
FFN_CHUNK = 256
KV_ROWS = 512


def _rms(x, g):
    y = x * lax.rsqrt(jnp.mean(x * x, axis=-1, keepdims=True) + RMS_EPS)
    return y * g


def _resident(shape):
    zeros = (0,) * len(shape)
    return pl.BlockSpec(shape, lambda *_: zeros, pipeline_mode=pl.Buffered(1))


def _params(semantics, vmem_bytes):
    return pltpu.CompilerParams(dimension_semantics=semantics, vmem_limit_bytes=int(vmem_bytes))


def _mem_kv_kernel(mem_ref, g_ref, w_ref, kv_ref):
    n = _rms(mem_ref[...], g_ref[...]).astype(BF16)
    kv_ref[...] = jnp.dot(n, w_ref[...], preferred_element_type=F32).astype(BF16)


def _mem_kv(mem2d, g, w_kv):
    rows, d = mem2d.shape
    cols = w_kv.shape[1]
    return pl.pallas_call(
        _mem_kv_kernel,
        out_shape=jax.ShapeDtypeStruct((rows, cols), BF16),
        grid=(rows // KV_ROWS,),
        in_specs=[pl.BlockSpec((KV_ROWS, d), lambda i: (i, 0)), _resident((1, d)), _resident((d, cols))],
        out_specs=pl.BlockSpec((KV_ROWS, cols), lambda i: (i, 0)),
        compiler_params=_params(("parallel",), 24 * MIB),
        name="mem_kv",
    )(mem2d, g, w_kv)


def _in_proj_kernel(x_ref, pos_ref, g_ref, w_ref, freq_ref,
                    qa_ref, ka_ref, va_ref, qb_ref, kb_ref, vb_ref, ga_ref, gb_ref):
    n = _rms(x_ref[...], g_ref[...]).astype(BF16)

    ang = pos_ref[...] * freq_ref[...]
    cos = jnp.cos(ang)
    sin = jnp.sin(ang)
    lane = lax.broadcasted_iota(jnp.int32, (1, LANES), 1) % HEAD_DIM
    half = ROPE_DIM // 2
    sin_lo = jnp.where(lane < half, -sin, 0.0)
    sin_hi = jnp.where((lane >= half) & (lane < ROPE_DIM), sin, 0.0)

    def proj(col, width):
        return jnp.dot(n, w_ref[:, col:col + width], preferred_element_type=F32)

    def rope(t):
        tiles = []
        for c in range(MIX_WIDTH // LANES):
            tc = t[:, c * LANES:(c + 1) * LANES]
            tiles.append(tc * cos + pltpu.roll(tc, LANES - half, 1) * sin_lo + pltpu.roll(tc, half, 1) * sin_hi)
        return jnp.concatenate(tiles, axis=1)

    scale = HEAD_DIM ** -0.5
    w = MIX_WIDTH
    qa_ref[...] = (rope(proj(0, w)) * scale).astype(BF16)
    ka_ref[...] = rope(proj(w, w)).astype(BF16)
    va_ref[...] = proj(2 * w, w).astype(BF16)
    qb_ref[...] = (proj(3 * w, w) * scale).astype(BF16)
    kb_ref[...] = proj(4 * w, w).astype(BF16)
    vb_ref[...] = proj(5 * w, w).astype(BF16)
    d = ga_ref.shape[1]
    for out_ref, col in ((ga_ref, 6 * w), (gb_ref, 6 * w + d)):
        for c in range(0, d, w):
            gate = proj(col + c, w)
            out_ref[:, c:c + w] = (1.0 / (1.0 + jnp.exp(-gate))).astype(BF16)


def _in_proj(x2d, pos_col, g, w_in, freq):
    rows, d = x2d.shape
    tm = PROJ_ROWS
    row_block = lambda width: pl.BlockSpec((tm, width), lambda i: (i, 0))
    out_shape = [jax.ShapeDtypeStruct((rows, MIX_WIDTH), BF16)] * 6 + [jax.ShapeDtypeStruct((rows, d), BF16)] * 2
    return pl.pallas_call(
        _in_proj_kernel,
        out_shape=out_shape,
        grid=(rows // tm,),
        in_specs=[row_block(d), row_block(1), _resident((1, d)), _resident(w_in.shape), _resident((1, LANES))],
        out_specs=[row_block(MIX_WIDTH)] * 6 + [row_block(d)] * 2,
        compiler_params=_params(("parallel",), 40 * MIB),
        name="in_proj",
    )(x2d, pos_col, g, w_in, freq)


def _split_heads(q):
    first = lax.broadcasted_iota(jnp.int32, (1, LANES), 1) < HEAD_DIM
    zero = jnp.zeros_like(q)
    return first, (jnp.where(first, q, zero), jnp.where(first, zero, q))


def _scores(qh, kj):
    return lax.dot_general(qh, kj, (((1,), (1,)), ((), ())), preferred_element_type=F32)


def _dil_attn_kernel(q_ref, k_ref, v_ref, bias_ref, o_ref):
    blk = ATTN_BLOCK
    i = pl.program_id(2)
    first, qh = _split_heads(q_ref[...])

    def body(j, carry):
        start = pl.multiple_of(j * blk, blk)
        kj = k_ref[pl.ds(start, blk), :]
        vj = v_ref[pl.ds(start, blk), :]
        bias = bias_ref[i - j]
        new = []
        for h in range(2):
            m, l, acc = carry[h]
            s = _scores(qh[h], kj) + bias
            m_new = jnp.maximum(m, jnp.max(s, axis=-1, keepdims=True))
            alpha = jnp.exp(m - m_new)
            p = jnp.exp(s - m_new)
            l = alpha * l + jnp.sum(p, axis=-1, keepdims=True)
            acc = alpha * acc + jnp.dot(p.astype(BF16), vj, preferred_element_type=F32)
            new.append((m_new, l, acc))
        return tuple(new)

    init = (jnp.full((blk, 1), MASKED, F32), jnp.zeros((blk, 1), F32), jnp.zeros((blk, LANES), F32))
    (_, l0, acc0), (_, l1, acc1) = lax.fori_loop(0, i + 1, body, (init, init))
    o_ref[...] = jnp.where(first, acc0 / l0, acc1 / l1).astype(BF16)


def _sb_attn_kernel(q_ref, k_ref, v_ref, tri_ref, mask_ref, o_ref):
    blk = ATTN_BLOCK
    i = pl.program_id(2)
    first, qh = _split_heads(q_ref[...])
    tri = tri_ref[...]

    def step(j, carry, diagonal):
        start = pl.multiple_of(j * blk, blk)
        kj = k_ref[pl.ds(start, blk), :]
        vj = v_ref[pl.ds(start, blk), :]
        new = []
        for h in range(2):
            c, acc = carry[h]
            z = _scores(qh[h], kj)
            sp = jnp.maximum(z, 0.0) + jnp.log1p(jnp.exp(-jnp.abs(z)))
            log_sig = z - sp
            if diagonal:
                sp = sp * mask_ref[...]
            sp_hi = sp.astype(BF16)
            sp_lo = (sp - sp_hi.astype(F32)).astype(BF16)
            sums = (jnp.dot(sp_hi, tri, preferred_element_type=F32)
                    + jnp.dot(sp_lo, tri, preferred_element_type=F32))
            within, total = sums[:, :blk], sums[:, blk:]
            a = jnp.exp(log_sig - within - jnp.concatenate([c] * (blk // LANES), axis=1))
            if diagonal:
                a = a * mask_ref[...]
            acc = acc + jnp.dot(a.astype(BF16), vj, preferred_element_type=F32)
            new.append((c + total, acc))
        return tuple(new)

    zero = jnp.zeros((blk, LANES), F32)
    carry = step(i, ((zero, zero), (zero, zero)), True)
    carry = lax.fori_loop(0, i, lambda jj, cr: step(i - 1 - jj, cr, False), carry)
    o_ref[...] = jnp.where(first, carry[0][1], carry[1][1]).astype(BF16)


def _mixer_call(kernel, name, q, k, v, consts, batch, seq):
    blk = ATTN_BLOCK
    nq = seq // blk
    pairs = MIX_WIDTH // LANES
    q_spec = pl.BlockSpec((blk, LANES), lambda b, p, i: (b * nq + i, p))
    kv_spec = pl.BlockSpec((seq, LANES), lambda b, p, i: (b, p))
    return pl.pallas_call(
        kernel,
        out_shape=jax.ShapeDtypeStruct(q.shape, BF16),
        grid=(batch, pairs, nq),
        in_specs=[q_spec, kv_spec, kv_spec] + [_resident(c.shape) for c in consts],
        out_specs=q_spec,
        compiler_params=_params(("parallel", "parallel", "arbitrary"), 32 * MIB),
        name=name,
    )(q, k, v, *consts)


def _dilated_bias(seq):
    blk = ATTN_BLOCK
    kd = np.arange(seq // blk)[:, None, None]
    dist = kd * blk + np.arange(blk)[None, :, None] - np.arange(blk)[None, None, :]
    count = np.zeros(dist.shape, np.float64)
    for window, dilation in DIL_PATTERNS:
        count += (dist >= 0) & (dist <= window) & (dist % dilation == 0)
    return np.where(count > 0, np.log(np.maximum(count, 1.0)), MASKED).astype(np.float32)


def _sb_constants():
    blk = ATTN_BLOCK
    later = np.arange(blk)[:, None] > np.arange(blk)[None, :]
    tri = np.concatenate([later, np.ones((blk, LANES), bool)], axis=1)
    strict = np.arange(blk)[None, :] < np.arange(blk)[:, None]
    return jnp.asarray(tri, BF16), jnp.asarray(strict, F32)


def _mix_mem_kernel(x_ref, oa_ref, ob_ref, ga_ref, gb_ref, wua_ref, wub_ref, wout_ref,
                    gq_ref, wq_ref, kv_ref, wo_ref, h_ref):
    ua = jnp.dot(oa_ref[...], wua_ref[...], preferred_element_type=F32)
    ub = jnp.dot(ob_ref[...], wub_ref[...], preferred_element_type=F32)
    mixed = ga_ref[...].astype(F32) * ua + gb_ref[...].astype(F32) * ub
    h = x_ref[...] + jnp.dot(mixed.astype(BF16), wout_ref[...], preferred_element_type=F32)

    hn = _rms(h, gq_ref[...]).astype(BF16)
    q = jnp.dot(hn, wq_ref[...], preferred_element_type=F32).astype(BF16)
    heads = []
    for hd in range(N_HEADS_MEM):
        lo, hi = hd * MEM_HEAD_DIM, (hd + 1) * MEM_HEAD_DIM
        s = _scores(q[:, lo:hi], kv_ref[:, lo:hi]) * (MEM_HEAD_DIM ** -0.5)
        p = jnp.exp(s - jnp.max(s, axis=-1, keepdims=True))
        o = jnp.dot(p.astype(BF16), kv_ref[:, MEM_WIDTH + lo:MEM_WIDTH + hi], preferred_element_type=F32)
        heads.append(o / jnp.sum(p, axis=-1, keepdims=True))
    o = jnp.concatenate(heads, axis=1).astype(BF16)
    h_ref[...] = h + jnp.dot(o, wo_ref[...], preferred_element_type=F32)


def _mix_mem(x2d, oa, ob, ga, gb, w_up_a, w_up_b, w_out, g_q, w_q, kv, w_o, seq, n_mem):
    rows, d = x2d.shape
    tm = MIX_ROWS
    per_seq = seq // tm
    row_block = lambda width: pl.BlockSpec((tm, width), lambda i: (i, 0))
    kv_spec = pl.BlockSpec((n_mem, kv.shape[1]), lambda i: (i // per_seq, 0))
    return pl.pallas_call(
        _mix_mem_kernel,
        out_shape=jax.ShapeDtypeStruct((rows, d), F32),
        grid=(rows // tm,),
        in_specs=[row_block(d), row_block(MIX_WIDTH), row_block(MIX_WIDTH), row_block(d), row_block(d),
                  _resident(w_up_a.shape), _resident(w_up_b.shape), _resident(w_out.shape),
                  _resident((1, d)), _resident(w_q.shape), kv_spec, _resident(w_o.shape)],
        out_specs=row_block(d),
        compiler_params=_params(("parallel",), 40 * MIB),
        name="mix_mem",
    )(x2d, oa, ob, ga, gb, w_up_a, w_up_b, w_out, g_q, w_q, kv, w_o)


def _ffn_kernel(h_ref, g_ref, wg_ref, wu_ref, wd_ref, gf_ref, o_ref, *, final_norm):
    h = h_ref[...]
    n = _rms(h, g_ref[...]).astype(BF16)
    acc = jnp.zeros(h.shape, F32)
    for c in range(wg_ref.shape[0]):
        gate = jnp.dot(n, wg_ref[c], preferred_element_type=F32)
        up = jnp.dot(n, wu_ref[c], preferred_element_type=F32)
        act = (gate / (1.0 + jnp.exp(-gate)) * up).astype(BF16)
        acc = acc + jnp.dot(act, wd_ref[c], preferred_element_type=F32)
    h = h + acc
    o_ref[...] = _rms(h, gf_ref[...]) if final_norm else h


def _ffn(h2d, g, w_gate, w_up, w_down, g_final, final_norm):
    rows, d = h2d.shape
    tm = FFN_ROWS
    row_block = pl.BlockSpec((tm, d), lambda i: (i, 0))
    return pl.pallas_call(
        functools.partial(_ffn_kernel, final_norm=final_norm),
        out_shape=jax.ShapeDtypeStruct((rows, d), F32),
        grid=(rows // tm,),
        in_specs=[row_block, _resident((1, d)), _resident(w_gate.shape), _resident(w_up.shape),
                  _resident(w_down.shape), _resident((1, d))],
        out_specs=row_block,
        compiler_params=_params(("parallel",), 48 * MIB),
        name="ffn",
    )(h2d, g, w_gate, w_up, w_down, g_final)


def kernel(x, mem, positions, g_mix, w_in, w_up_a, w_up_b, w_out, g_mem_q, g_mem_kv,
           w_q_mem, w_kv_mem, w_o_mem, g_ffn, w_ffn_gate, w_ffn_up, w_ffn_down, g_final):
    batch, seq, d = x.shape
    n_mem = mem.shape[1]
    depth = w_in.shape[0]
    d_ff = w_ffn_gate.shape[-1]
    assert seq % ATTN_BLOCK == 0 and seq % MIX_ROWS == 0 and d_ff % FFN_CHUNK == 0
    assert w_in.shape[-1] == 6 * MIX_WIDTH + 2 * d and w_kv_mem.shape[-1] == 2 * MEM_WIDTH

    row = lambda g: g.reshape(1, d)
    h = x.reshape(batch * seq, d)
    pos_col = positions.astype(F32).reshape(batch * seq, 1)
    mem2d = mem.reshape(batch * n_mem, d)

    half = ROPE_DIM // 2
    inv_freq = ROPE_THETA ** (-jnp.arange(half, dtype=F32) / half)
    lane = np.arange(LANES) % HEAD_DIM
    freq = jnp.where(lane < ROPE_DIM, inv_freq[lane % half], 0.0).reshape(1, LANES)
    bias = jnp.asarray(_dilated_bias(seq))
    tri, strict = _sb_constants()
    chunks = d_ff // FFN_CHUNK

    for l in range(depth):
        qa, ka, va, qb, kb, vb, ga, gb = _in_proj(h, pos_col, row(g_mix[l]), w_in[l].astype(BF16), freq)
        oa = _mixer_call(_dil_attn_kernel, "dil_attn", qa, ka, va, (bias,), batch, seq)
        ob = _mixer_call(_sb_attn_kernel, "sb_attn", qb, kb, vb, (tri, strict), batch, seq)
        kv = _mem_kv(mem2d, row(g_mem_kv[l]), w_kv_mem[l].astype(BF16))
        h = _mix_mem(h, oa, ob, ga, gb, w_up_a[l].astype(BF16), w_up_b[l].astype(BF16), w_out[l].astype(BF16),
                     row(g_mem_q[l]), w_q_mem[l].astype(BF16), kv, w_o_mem[l].astype(BF16), seq, n_mem)
        w_gate = w_ffn_gate[l].astype(BF16).reshape(d, chunks, FFN_CHUNK).transpose(1, 0, 2)
        w_up = w_ffn_up[l].astype(BF16).reshape(d, chunks, FFN_CHUNK).transpose(1, 0, 2)
        w_down = w_ffn_down[l].astype(BF16).reshape(chunks, FFN_CHUNK, d)
        h = _ffn(h, row(g_ffn[l]), w_gate, w_up, w_down, row(g_final), final_norm=(l == depth - 1))
    return h.reshape(batch, seq, d)
```

```python
import functools

import numpy as np
import jax
import jax.numpy as jnp
from jax import lax
from jax.experimental import pallas as pl
from jax.experimental.pallas import tpu as pltpu

F32 = jnp.float32
BF16 = jnp.bfloat16

HEAD_DIM = 64
MIX_WIDTH = 512
DIL_PATTERNS = ((128, 1), (512, 4), (2048, 16))
ROPE_THETA = 500000.0
ROPE_DIM = HEAD_DIM // 4
N_HEADS_MEM = 4
MEM_HEAD_DIM = 128
MEM_WIDTH = N_HEADS_MEM * MEM_HEAD_DIM
RMS_EPS = 1e-6

LANES = 128
MIB = 1024 * 1024
MASKED = -1e30
SB_DEAD_MASS = 110.0

ATTN_BLOCK = 256
PROJ_ROWS = 256
MIX_ROWS = 256
FFN_ROWS = 512
FFN_CHUNK = 256
KV_ROWS = 512


def _rms(x, g):
    y = x * lax.rsqrt(jnp.mean(x * x, axis=-1, keepdims=True) + RMS_EPS)
    return y * g


def _resident(shape):
    zeros = (0,) * len(shape)
    return pl.BlockSpec(shape, lambda *_: zeros, pipeline_mode=pl.Buffered(1))


def _params(semantics, vmem_bytes):
    return pltpu.CompilerParams(dimension_semantics=semantics, vmem_limit_bytes=int(vmem_bytes))


def _mem_kv_kernel(mem_ref, g_ref, w_ref, kv_ref):
    n = _rms(mem_ref[...], g_ref[...]).astype(BF16)
    kv_ref[...] = jnp.dot(n, w_ref[...], preferred_element_type=F32).astype(BF16)


def _mem_kv(mem2d, g, w_kv):
    rows, d = mem2d.shape
    cols = w_kv.shape[1]
    return pl.pallas_call(
        _mem_kv_kernel,
        out_shape=jax.ShapeDtypeStruct((rows, cols), BF16),
        grid=(rows // KV_ROWS,),
        in_specs=[pl.BlockSpec((KV_ROWS, d), lambda i: (i, 0)), _resident((1, d)), _resident((d, cols))],
        out_specs=pl.BlockSpec((KV_ROWS, cols), lambda i: (i, 0)),
        compiler_params=_params(("parallel",), 24 * MIB),
        name="mem_kv",
    )(mem2d, g, w_kv)


def _in_proj_kernel(x_ref, pos_ref, g_ref, w_ref, freq_ref,
                    qa_ref, ka_ref, va_ref, qb_ref, kb_ref, vb_ref, ga_ref, gb_ref):
    n = _rms(x_ref[...], g_ref[...]).astype(BF16)

    ang = pos_ref[...] * freq_ref[...]
    cos = jnp.cos(ang)
    sin = jnp.sin(ang)
    lane = lax.broadcasted_iota(jnp.int32, (1, LANES), 1) % HEAD_DIM
    half = ROPE_DIM // 2
    sin_lo = jnp.where(lane < half, -sin, 0.0)
    sin_hi = jnp.where((lane >= half) & (lane < ROPE_DIM), sin, 0.0)

    def proj(col, width):
        return jnp.dot(n, w_ref[:, col:col + width], preferred_element_type=F32)

    def rope(t):
        tiles = []
        for c in range(MIX_WIDTH // LANES):
            tc = t[:, c * LANES:(c + 1) * LANES]
            tiles.append(tc * cos + pltpu.roll(tc, LANES - half, 1) * sin_lo + pltpu.roll(tc, half, 1) * sin_hi)
        return jnp.concatenate(tiles, axis=1)

    scale = HEAD_DIM ** -0.5
    w = MIX_WIDTH
    qa_ref[...] = (rope(proj(0, w)) * scale).astype(BF16)
    ka_ref[...] = rope(proj(w, w)).astype(BF16)
    va_ref[...] = proj(2 * w, w).astype(BF16)
    qb_ref[...] = (proj(3 * w, w) * scale).astype(BF16)
    kb_ref[...] = proj(4 * w, w).astype(BF16)
    vb_ref[...] = proj(5 * w, w).astype(BF16)
    d = ga_ref.shape[1]
    for out_ref, col in ((ga_ref, 6 * w), (gb_ref, 6 * w + d)):
        for c in range(0, d, w):
            gate = proj(col + c, w)
            out_ref[:, c:c + w] = (1.0 / (1.0 + jnp.exp(-gate))).astype(BF16)


def _in_proj(x2d, pos_col, g, w_in, freq):
    rows, d = x2d.shape
    tm = PROJ_ROWS
    row_block = lambda width: pl.BlockSpec((tm, width), lambda i: (i, 0))
    out_shape = [jax.ShapeDtypeStruct((rows, MIX_WIDTH), BF16)] * 6 + [jax.ShapeDtypeStruct((rows, d), BF16)] * 2
    return pl.pallas_call(
        _in_proj_kernel,
        out_shape=out_shape,
        grid=(rows // tm,),
        in_specs=[row_block(d), row_block(1), _resident((1, d)), _resident(w_in.shape), _resident((1, LANES))],
        out_specs=[row_block(MIX_WIDTH)] * 6 + [row_block(d)] * 2,
        compiler_params=_params(("parallel",), 40 * MIB),
        name="in_proj",
    )(x2d, pos_col, g, w_in, freq)


def _split_heads(q):
    first = lax.broadcasted_iota(jnp.int32, (1, LANES), 1) < HEAD_DIM
    zero = jnp.zeros_like(q)
    return first, (jnp.where(first, q, zero), jnp.where(first, zero, q))


def _scores(qh, kj):
    return lax.dot_general(qh, kj, (((1,), (1,)), ((), ())), preferred_element_type=F32)


def _dil_attn_kernel(q_ref, k_ref, v_ref, bias_ref, o_ref, vones_ref, s_ref):
    blk = ATTN_BLOCK
    nq = k_ref.shape[0] // blk
    i = pl.program_id(2)

    @pl.when(i == 0)
    def _():
        vones_ref[:, :LANES] = v_ref[...]
        vones_ref[:, LANES:] = jnp.ones((k_ref.shape[0], LANES), BF16)

    def sweep(n_keys):
        first, qh = _split_heads(q_ref[...])
        row_max = [None, None]
        for j in range(n_keys):
            kj = k_ref[j * blk:(j + 1) * blk, :]
            bias = bias_ref[n_keys - 1 - j]
            for h in range(2):
                s = _scores(qh[h], kj) + bias
                s_ref[h, j] = s
                part = jnp.maximum(s[:, :LANES], s[:, LANES:])
                row_max[h] = part if row_max[h] is None else jnp.maximum(row_max[h], part)
        outs = []
        for h in range(2):
            m = jnp.broadcast_to(jnp.max(row_max[h], axis=-1, keepdims=True), (blk, LANES))
            m = jnp.concatenate([m] * (blk // LANES), axis=1)
            acc = None
            for j in range(n_keys):
                p = jnp.exp(s_ref[h, j] - m).astype(BF16)
                pv = jnp.dot(p, vones_ref[j * blk:(j + 1) * blk, :], preferred_element_type=F32)
                acc = pv if acc is None else acc + pv
            outs.append(acc[:, :LANES] / acc[:, LANES:])
        o_ref[...] = jnp.where(first, outs[0], outs[1]).astype(BF16)

    for n in range(nq):
        pl.when(i == n)(functools.partial(sweep, n + 1))


def _sb_attn_kernel(q_ref, k_ref, v_ref, tri_ref, mask_ref, o_ref):
    blk = ATTN_BLOCK
    i = pl.program_id(2)
    first, qh = _split_heads(q_ref[...])
    tri = tri_ref[...]

    def step(j, carry, diagonal):
        start = pl.multiple_of(j * blk, blk)
        kj = k_ref[pl.ds(start, blk), :]
        vj = v_ref[pl.ds(start, blk), :]
        new = []
        for h in range(2):
            c, acc = carry[h]
            z = _scores(qh[h], kj)
            sp = jnp.maximum(z, 0.0) + jnp.log(1.0 + jnp.exp(-jnp.abs(z)))
            log_sig = z - sp
            if diagonal:
                sp = sp * mask_ref[...]
            sp_hi = sp.astype(BF16)
            sp_lo = (sp - sp_hi.astype(F32)).astype(BF16)
            within = (jnp.dot(sp_hi, tri, preferred_element_type=F32)
                      + jnp.dot(sp_lo, tri, preferred_element_type=F32))
            a = jnp.exp(log_sig - within - jnp.concatenate([c] * (blk // LANES), axis=1))
            if diagonal:
                a = a * mask_ref[...]
            acc = acc + jnp.dot(a.astype(BF16), vj, preferred_element_type=F32)
            total = jnp.broadcast_to(within[:, :1] + sp[:, :1], (blk, LANES))
            new.append((c + total, acc))
        return tuple(new)

    def body(state):
        j, _, carry = state
        carry = step(j, carry, False)
        c_min = jnp.min(jnp.minimum(carry[0][0], carry[1][0]))
        return j - 1, (j > 0) & (c_min <= SB_DEAD_MASS), carry

    zero = jnp.zeros((blk, LANES), F32)
    carry = step(i, ((zero, zero), (zero, zero)), True)
    _, _, carry = lax.while_loop(lambda state: state[1], body, (i - 1, i > 0, carry))
    o_ref[...] = jnp.where(first, carry[0][1], carry[1][1]).astype(BF16)


def _mixer_call(kernel, name, q, k, v, consts, scratch, batch, seq):
    blk = ATTN_BLOCK
    nq = seq // blk
    pairs = MIX_WIDTH // LANES
    q_spec = pl.BlockSpec((blk, LANES), lambda b, p, i: (b * nq + i, p))
    kv_spec = pl.BlockSpec((seq, LANES), lambda b, p, i: (b, p))
    return pl.pallas_call(
        kernel,
        out_shape=jax.ShapeDtypeStruct(q.shape, BF16),
        grid=(batch, pairs, nq),
        in_specs=[q_spec, kv_spec, kv_spec] + [_resident(c.shape) for c in consts],
        out_specs=q_spec,
        scratch_shapes=scratch,
        compiler_params=_params(("parallel", "parallel", "arbitrary"), 32 * MIB),
        name=name,
    )(q, k, v, *consts)


def _dilated_bias(seq):
    blk = ATTN_BLOCK
    kd = np.arange(seq // blk)[:, None, None]
    dist = kd * blk + np.arange(blk)[None, :, None] - np.arange(blk)[None, None, :]
    count = np.zeros(dist.shape, np.float64)
    for window, dilation in DIL_PATTERNS:
        count += (dist >= 0) & (dist <= window) & (dist % dilation == 0)
    return np.where(count > 0, np.log(np.maximum(count, 1.0)), MASKED).astype(np.float32)


def _sb_constants():
    blk = ATTN_BLOCK
    tri = np.arange(blk)[:, None] > np.arange(blk)[None, :]
    strict = np.arange(blk)[None, :] < np.arange(blk)[:, None]
    return jnp.asarray(tri, BF16), jnp.asarray(strict, F32)


def _mix_mem_kernel(x_ref, oa_ref, ob_ref, ga_ref, gb_ref, wua_ref, wub_ref, wout_ref,
                    gq_ref, wq_ref, kv_ref, wo_ref, h_ref):
    ua = jnp.dot(oa_ref[...], wua_ref[...], preferred_element_type=F32)
    ub = jnp.dot(ob_ref[...], wub_ref[...], preferred_element_type=F32)
    mixed = ga_ref[...].astype(F32) * ua + gb_ref[...].astype(F32) * ub
    h = x_ref[...] + jnp.dot(mixed.astype(BF16), wout_ref[...], preferred_element_type=F32)

    hn = _rms(h, gq_ref[...]).astype(BF16)
    q = jnp.dot(hn, wq_ref[...], preferred_element_type=F32).astype(BF16)
    heads = []
    for hd in range(N_HEADS_MEM):
        lo, hi = hd * MEM_HEAD_DIM, (hd + 1) * MEM_HEAD_DIM
        s = _scores(q[:, lo:hi], kv_ref[:, lo:hi]) * (MEM_HEAD_DIM ** -0.5)
        p = jnp.exp(s - jnp.max(s, axis=-1, keepdims=True))
        o = jnp.dot(p.astype(BF16), kv_ref[:, MEM_WIDTH + lo:MEM_WIDTH + hi], preferred_element_type=F32)
        heads.append(o / jnp.sum(p, axis=-1, keepdims=True))
    o = jnp.concatenate(heads, axis=1).astype(BF16)
    h_ref[...] = h + jnp.dot(o, wo_ref[...], preferred_element_type=F32)


def _mix_mem(x2d, oa, ob, ga, gb, w_up_a, w_up_b, w_out, g_q, w_q, kv, w_o, seq, n_mem):
    rows, d = x2d.shape
    tm = MIX_ROWS
    per_seq = seq // tm
    row_block = lambda width: pl.BlockSpec((tm, width), lambda i: (i, 0))
    kv_spec = pl.BlockSpec((n_mem, kv.shape[1]), lambda i: (i // per_seq, 0))
    return pl.pallas_call(
        _mix_mem_kernel,
        out_shape=jax.ShapeDtypeStruct((rows, d), F32),
        grid=(rows // tm,),
        in_specs=[row_block(d), row_block(MIX_WIDTH), row_block(MIX_WIDTH), row_block(d), row_block(d),
                  _resident(w_up_a.shape), _resident(w_up_b.shape), _resident(w_out.shape),
                  _resident((1, d)), _resident(w_q.shape), kv_spec, _resident(w_o.shape)],
        out_specs=row_block(d),
        compiler_params=_params(("parallel",), 40 * MIB),
        name="mix_mem",
    )(x2d, oa, ob, ga, gb, w_up_a, w_up_b, w_out, g_q, w_q, kv, w_o)


def _ffn_kernel(h_ref, g_ref, wg_ref, wu_ref, wd_ref, gf_ref, o_ref, *, final_norm):
    h = h_ref[...]
    n = _rms(h, g_ref[...]).astype(BF16)
    acc = jnp.zeros(h.shape, F32)
    for c in range(wg_ref.shape[0]):
        gate = jnp.dot(n, wg_ref[c], preferred_element_type=F32)
        up = jnp.dot(n, wu_ref[c], preferred_element_type=F32)
        act = (gate / (1.0 + jnp.exp(-gate)) * up).astype(BF16)
        acc = acc + jnp.dot(act, wd_ref[c], preferred_element_type=F32)
    h = h + acc
    o_ref[...] = _rms(h, gf_ref[...]) if final_norm else h


def _ffn(h2d, g, w_gate, w_up, w_down, g_final, final_norm):
    rows, d = h2d.shape
    tm = FFN_ROWS
    row_block = pl.BlockSpec((tm, d), lambda i: (i, 0))
    return pl.pallas_call(
        functools.partial(_ffn_kernel, final_norm=final_norm),
        out_shape=jax.ShapeDtypeStruct((rows, d), F32),
        grid=(rows // tm,),
        in_specs=[row_block, _resident((1, d)), _resident(w_gate.shape), _resident(w_up.shape),
                  _resident(w_down.shape), _resident((1, d))],
        out_specs=row_block,
        compiler_params=_params(("parallel",), 48 * MIB),
        name="ffn",
    )(h2d, g, w_gate, w_up, w_down, g_final)


def kernel(x, mem, positions, g_mix, w_in, w_up_a, w_up_b, w_out, g_mem_q, g_mem_kv,
           w_q_mem, w_kv_mem, w_o_mem, g_ffn, w_ffn_gate, w_ffn_up, w_ffn_down, g_final):
    batch, seq, d = x.shape
    n_mem = mem.shape[1]
    depth = w_in.shape[0]
    d_ff = w_ffn_gate.shape[-1]
    assert seq % ATTN_BLOCK == 0 and seq % MIX_ROWS == 0 and d_ff % FFN_CHUNK == 0
    assert w_in.shape[-1] == 6 * MIX_WIDTH + 2 * d and w_kv_mem.shape[-1] == 2 * MEM_WIDTH

    row = lambda g: g.reshape(1, d)
    h = x.reshape(batch * seq, d)
    pos_col = positions.astype(F32).reshape(batch * seq, 1)
    mem2d = mem.reshape(batch * n_mem, d)

    half = ROPE_DIM // 2
    inv_freq = ROPE_THETA ** (-jnp.arange(half, dtype=F32) / half)
    lane = np.arange(LANES) % HEAD_DIM
    freq = jnp.where(lane < ROPE_DIM, inv_freq[lane % half], 0.0).reshape(1, LANES)
    bias = jnp.asarray(_dilated_bias(seq))
    tri, strict = _sb_constants()
    chunks = d_ff // FFN_CHUNK

    for l in range(depth):
        qa, ka, va, qb, kb, vb, ga, gb = _in_proj(h, pos_col, row(g_mix[l]), w_in[l].astype(BF16), freq)
        dil_scratch = [pltpu.VMEM((seq, 2 * LANES), BF16),
                       pltpu.VMEM((2, seq // ATTN_BLOCK, ATTN_BLOCK, ATTN_BLOCK), F32)]
        oa = _mixer_call(_dil_attn_kernel, "dil_attn", qa, ka, va, (bias,), dil_scratch, batch, seq)
        ob = _mixer_call(_sb_attn_kernel, "sb_attn", qb, kb, vb, (tri, strict), [], batch, seq)
        kv = _mem_kv(mem2d, row(g_mem_kv[l]), w_kv_mem[l].astype(BF16))
        h = _mix_mem(h, oa, ob, ga, gb, w_up_a[l].astype(BF16), w_up_b[l].astype(BF16), w_out[l].astype(BF16),
                     row(g_mem_q[l]), w_q_mem[l].astype(BF16), kv, w_o_mem[l].astype(BF16), seq, n_mem)
        w_gate = w_ffn_gate[l].astype(BF16).reshape(d, chunks, FFN_CHUNK).transpose(1, 0, 2)
        w_up = w_ffn_up[l].astype(BF16).reshape(d, chunks, FFN_CHUNK).transpose(1, 0, 2)
        w_down = w_ffn_down[l].astype(BF16).reshape(chunks, FFN_CHUNK, d)
        h = _ffn(h, row(g_ffn[l]), w_gate, w_up, w_down, row(g_final), final_norm=(l == depth - 1))
    return h.reshape(batch, seq, d)
```

```python
import functools

import numpy as np
import jax
import jax.numpy as jnp
from jax import lax
from jax.experimental import pallas as pl
from jax.experimental.pallas import tpu as pltpu

F32 = jnp.float32
BF16 = jnp.bfloat16

HEAD_DIM = 64
MIX_WIDTH = 512
DIL_PATTERNS = ((128, 1), (512, 4), (2048, 16))
ROPE_THETA = 500000.0
ROPE_DIM = HEAD_DIM // 4
N_HEADS_MEM = 4
MEM_HEAD_DIM = 128
MEM_WIDTH = N_HEADS_MEM * MEM_HEAD_DIM
RMS_EPS = 1e-6

LANES = 128
MIB = 1024 * 1024
MASKED = -1e30
SB_DEAD_MASS = 110.0

ATTN_BLOCK = 256
PROJ_ROWS = 512
MIX_ROWS = 512
FFN_ROWS = 512
FFN_CHUNK = 256
KV_ROWS = 512


def _rms(x, g):
    y = x * lax.rsqrt(jnp.mean(x * x, axis=-1, keepdims=True) + RMS_EPS)
    return y * g


def _resident(shape):
    zeros = (0,) * len(shape)
    return pl.BlockSpec(shape, lambda *_: zeros, pipeline_mode=pl.Buffered(1))


def _params(semantics, vmem_bytes):
    return pltpu.CompilerParams(dimension_semantics=semantics, vmem_limit_bytes=int(vmem_bytes))


def _mem_kv_kernel(mem_ref, g_ref, w_ref, kv_ref):
    n = _rms(mem_ref[...], g_ref[...]).astype(BF16)
    kv_ref[...] = jnp.dot(n, w_ref[...], preferred_element_type=F32).astype(BF16)


def _mem_kv(mem2d, g, w_kv):
    rows, d = mem2d.shape
    cols = w_kv.shape[1]
    return pl.pallas_call(
        _mem_kv_kernel,
        out_shape=jax.ShapeDtypeStruct((rows, cols), BF16),
        grid=(rows // KV_ROWS,),
        in_specs=[pl.BlockSpec((KV_ROWS, d), lambda i: (i, 0)), _resident((1, d)), _resident((d, cols))],
        out_specs=pl.BlockSpec((KV_ROWS, cols), lambda i: (i, 0)),
        compiler_params=_params(("parallel",), 24 * MIB),
        name="mem_kv",
    )(mem2d, g, w_kv)


def _in_proj_kernel(x_ref, pos_ref, g_ref, w_ref, freq_ref,
                    qa_ref, ka_ref, va_ref, qb_ref, kb_ref, vb_ref, ga_ref, gb_ref):
    n = _rms(x_ref[...], g_ref[...]).astype(BF16)

    ang = pos_ref[...] * freq_ref[...]
    cos = jnp.cos(ang)
    sin = jnp.sin(ang)
    lane = lax.broadcasted_iota(jnp.int32, (1, LANES), 1) % HEAD_DIM
    half = ROPE_DIM // 2
    sin_lo = jnp.where(lane < half, -sin, 0.0)
    sin_hi = jnp.where((lane >= half) & (lane < ROPE_DIM), sin, 0.0)

    def proj(col, width):
        return jnp.dot(n, w_ref[:, col:col + width], preferred_element_type=F32)

    def rope(t):
        tiles = []
        for c in range(MIX_WIDTH // LANES):
            tc = t[:, c * LANES:(c + 1) * LANES]
            tiles.append(tc * cos + pltpu.roll(tc, LANES - half, 1) * sin_lo + pltpu.roll(tc, half, 1) * sin_hi)
        return jnp.concatenate(tiles, axis=1)

    scale = HEAD_DIM ** -0.5
    w = MIX_WIDTH
    qa_ref[...] = (rope(proj(0, w)) * scale).astype(BF16)
    ka_ref[...] = rope(proj(w, w)).astype(BF16)
    va_ref[...] = proj(2 * w, w).astype(BF16)
    qb_ref[...] = (proj(3 * w, w) * scale).astype(BF16)
    kb_ref[...] = proj(4 * w, w).astype(BF16)
    vb_ref[...] = proj(5 * w, w).astype(BF16)
    d = ga_ref.shape[1]
    for out_ref, col in ((ga_ref, 6 * w), (gb_ref, 6 * w + d)):
        for c in range(0, d, w):
            gate = proj(col + c, w)
            out_ref[:, c:c + w] = (1.0 / (1.0 + jnp.exp(-gate))).astype(BF16)


def _in_proj(x2d, pos_col, g, w_in, freq):
    rows, d = x2d.shape
    tm = PROJ_ROWS
    row_block = lambda width: pl.BlockSpec((tm, width), lambda i: (i, 0))
    out_shape = [jax.ShapeDtypeStruct((rows, MIX_WIDTH), BF16)] * 6 + [jax.ShapeDtypeStruct((rows, d), BF16)] * 2
    return pl.pallas_call(
        _in_proj_kernel,
        out_shape=out_shape,
        grid=(rows // tm,),
        in_specs=[row_block(d), row_block(1), _resident((1, d)), _resident(w_in.shape), _resident((1, LANES))],
        out_specs=[row_block(MIX_WIDTH)] * 6 + [row_block(d)] * 2,
        compiler_params=_params(("parallel",), 56 * MIB),
        name="in_proj",
    )(x2d, pos_col, g, w_in, freq)


def _split_heads(q):
    first = lax.broadcasted_iota(jnp.int32, (1, LANES), 1) < HEAD_DIM
    zero = jnp.zeros_like(q)
    return first, (jnp.where(first, q, zero), jnp.where(first, zero, q))


def _scores(qh, kj):
    return lax.dot_general(qh, kj, (((1,), (1,)), ((), ())), preferred_element_type=F32)


def _dil_attn_kernel(q_ref, k_ref, v_ref, bias_ref, o_ref, vones_ref, s_ref):
    blk = ATTN_BLOCK
    nq = k_ref.shape[0] // blk
    i = pl.program_id(2)

    @pl.when(i == 0)
    def _():
        vones_ref[:, :LANES] = v_ref[...]
        vones_ref[:, LANES:] = jnp.ones((k_ref.shape[0], LANES), BF16)

    def sweep(n_keys):
        first, qh = _split_heads(q_ref[...])
        row_max = [None, None]
        for j in range(n_keys):
            kj = k_ref[j * blk:(j + 1) * blk, :]
            bias = bias_ref[n_keys - 1 - j]
            for h in range(2):
                s = _scores(qh[h], kj) + bias
                s_ref[h, j] = s
                part = jnp.maximum(s[:, :LANES], s[:, LANES:])
                row_max[h] = part if row_max[h] is None else jnp.maximum(row_max[h], part)
        outs = []
        for h in range(2):
            m = jnp.broadcast_to(jnp.max(row_max[h], axis=-1, keepdims=True), (blk, LANES))
            m = jnp.concatenate([m] * (blk // LANES), axis=1)
            acc = None
            for j in range(n_keys):
                p = jnp.exp(s_ref[h, j] - m).astype(BF16)
                pv = jnp.dot(p, vones_ref[j * blk:(j + 1) * blk, :], preferred_element_type=F32)
                acc = pv if acc is None else acc + pv
            outs.append(acc[:, :LANES] / acc[:, LANES:])
        o_ref[...] = jnp.where(first, outs[0], outs[1]).astype(BF16)

    for n in range(nq):
        pl.when(i == n)(functools.partial(sweep, n + 1))


def _sb_attn_kernel(q_ref, k_ref, v_ref, tri_ref, mask_ref, o_ref,
                    c_ref, acc_ref, hi_ref, lo_ref, ls_ref, a_ref):
    blk = ATTN_BLOCK
    nq = q_ref.shape[0] // blk
    tri = tri_ref[...]

    def query_heads(i):
        return _split_heads(q_ref[i * blk:(i + 1) * blk, :])[1]

    def softplus_pass(i, h, qh, kj, diagonal):
        z = _scores(qh, kj)
        sp = jnp.maximum(z, 0.0) + jnp.log(1.0 + jnp.exp(-jnp.abs(z)))
        ls_ref[i, h] = z - sp
        if diagonal:
            sp = sp * mask_ref[...]
        hi = sp.astype(BF16)
        hi_ref[i, h] = hi
        lo_ref[i, h] = (sp - hi.astype(F32)).astype(BF16)
        return sp[:, :1]

    def weight_pass(i, h, first_col, c, diagonal):
        within = (jnp.dot(hi_ref[i, h], tri, preferred_element_type=F32)
                  + jnp.dot(lo_ref[i, h], tri, preferred_element_type=F32))
        logit = ls_ref[i, h] - within
        if c is not None:
            logit = logit - jnp.concatenate([c] * (blk // LANES), axis=1)
        a = jnp.exp(logit)
        if diagonal:
            a = a * mask_ref[...]
        a_ref[i, h] = a.astype(BF16)
        return jnp.broadcast_to(within[:, :1] + first_col, (blk, LANES))

    def diagonal_sweep(rows, key_start, d):
        first_col, total = {}, {}
        for i in rows:
            qh = query_heads(i)
            kj = k_ref[pl.ds(key_start(i), blk), :]
            for h in range(2):
                first_col[i, h] = softplus_pass(i, h, qh[h], kj, d is None)
        for i in rows:
            for h in range(2):
                c = None if d is None else c_ref[i, h]
                total[i, h] = weight_pass(i, h, first_col[i, h], c, d is None)
        pending_min = None
        for i in rows:
            vj = v_ref[pl.ds(key_start(i), blk), :]
            for h in range(2):
                pv = jnp.dot(a_ref[i, h], vj, preferred_element_type=F32)
                if d is None:
                    c_ref[i, h] = total[i, h]
                    acc_ref[i, h] = pv
                else:
                    live = (i >= d).astype(F32)
                    c = c_ref[i, h] + live * total[i, h]
                    c_ref[i, h] = c
                    acc_ref[i, h] += live * pv
                    pending = c + (i <= d).astype(F32) * (2.0 * SB_DEAD_MASS)
                    pending_min = pending if pending_min is None else jnp.minimum(pending_min, pending)
        return pending_min

    diagonal_sweep(range(nq), lambda i: i * blk, None)

    def body(state):
        d, _ = state
        start = lambda i: pl.multiple_of(jnp.maximum(i - d, 0) * blk, blk)
        pending_min = diagonal_sweep(range(1, nq), start, d)
        return d + 1, (d + 1 < nq) & (jnp.min(pending_min) <= SB_DEAD_MASS)

    lax.while_loop(lambda state: state[1], body, (jnp.int32(1), jnp.asarray(nq > 1)))
    first = lax.broadcasted_iota(jnp.int32, (1, LANES), 1) < HEAD_DIM
    for i in range(nq):
        o_ref[i * blk:(i + 1) * blk, :] = jnp.where(first, acc_ref[i, 0], acc_ref[i, 1]).astype(BF16)


def _sb_attn(q, k, v, tri, strict, batch, seq):
    nq = seq // ATTN_BLOCK
    spec = pl.BlockSpec((seq, LANES), lambda b, p: (b, p))
    state = pltpu.VMEM((nq, 2, ATTN_BLOCK, LANES), F32)
    tiles = lambda dtype: pltpu.VMEM((nq, 2, ATTN_BLOCK, ATTN_BLOCK), dtype)
    return pl.pallas_call(
        _sb_attn_kernel,
        out_shape=jax.ShapeDtypeStruct(q.shape, BF16),
        grid=(batch, MIX_WIDTH // LANES),
        in_specs=[spec, spec, spec, _resident(tri.shape), _resident(strict.shape)],
        out_specs=spec,
        scratch_shapes=[state, state, tiles(BF16), tiles(BF16), tiles(F32), tiles(BF16)],
        compiler_params=_params(("parallel", "parallel"), 48 * MIB),
        name="sb_attn",
    )(q, k, v, tri, strict)


def _mixer_call(kernel, name, q, k, v, consts, scratch, batch, seq):
    blk = ATTN_BLOCK
    nq = seq // blk
    pairs = MIX_WIDTH // LANES
    q_spec = pl.BlockSpec((blk, LANES), lambda b, p, i: (b * nq + i, p))
    kv_spec = pl.BlockSpec((seq, LANES), lambda b, p, i: (b, p))
    return pl.pallas_call(
        kernel,
        out_shape=jax.ShapeDtypeStruct(q.shape, BF16),
        grid=(batch, pairs, nq),
        in_specs=[q_spec, kv_spec, kv_spec] + [_resident(c.shape) for c in consts],
        out_specs=q_spec,
        scratch_shapes=scratch,
        compiler_params=_params(("parallel", "parallel", "arbitrary"), 32 * MIB),
        name=name,
    )(q, k, v, *consts)


def _dilated_bias(seq):
    blk = ATTN_BLOCK
    kd = np.arange(seq // blk)[:, None, None]
    dist = kd * blk + np.arange(blk)[None, :, None] - np.arange(blk)[None, None, :]
    count = np.zeros(dist.shape, np.float64)
    for window, dilation in DIL_PATTERNS:
        count += (dist >= 0) & (dist <= window) & (dist % dilation == 0)
    return np.where(count > 0, np.log(np.maximum(count, 1.0)), MASKED).astype(np.float32)


def _sb_constants():
    blk = ATTN_BLOCK
    tri = np.arange(blk)[:, None] > np.arange(blk)[None, :]
    strict = np.arange(blk)[None, :] < np.arange(blk)[:, None]
    return jnp.asarray(tri, BF16), jnp.asarray(strict, F32)


def _mix_mem_kernel(x_ref, oa_ref, ob_ref, ga_ref, gb_ref, wua_ref, wub_ref, wout_ref,
                    gq_ref, wq_ref, kv_ref, wo_ref, h_ref):
    ua = jnp.dot(oa_ref[...], wua_ref[...], preferred_element_type=F32)
    ub = jnp.dot(ob_ref[...], wub_ref[...], preferred_element_type=F32)
    mixed = ga_ref[...].astype(F32) * ua + gb_ref[...].astype(F32) * ub
    h = x_ref[...] + jnp.dot(mixed.astype(BF16), wout_ref[...], preferred_element_type=F32)

    hn = _rms(h, gq_ref[...]).astype(BF16)
    q = jnp.dot(hn, wq_ref[...], preferred_element_type=F32).astype(BF16)
    heads = []
    for hd in range(N_HEADS_MEM):
        lo, hi = hd * MEM_HEAD_DIM, (hd + 1) * MEM_HEAD_DIM
        s = _scores(q[:, lo:hi], kv_ref[:, lo:hi]) * (MEM_HEAD_DIM ** -0.5)
        p = jnp.exp(s - jnp.max(s, axis=-1, keepdims=True))
        o = jnp.dot(p.astype(BF16), kv_ref[:, MEM_WIDTH + lo:MEM_WIDTH + hi], preferred_element_type=F32)
        heads.append(o / jnp.sum(p, axis=-1, keepdims=True))
    o = jnp.concatenate(heads, axis=1).astype(BF16)
    h_ref[...] = h + jnp.dot(o, wo_ref[...], preferred_element_type=F32)


def _mix_mem(x2d, oa, ob, ga, gb, w_up_a, w_up_b, w_out, g_q, w_q, kv, w_o, seq, n_mem):
    rows, d = x2d.shape
    tm = MIX_ROWS
    per_seq = seq // tm
    row_block = lambda width: pl.BlockSpec((tm, width), lambda i: (i, 0))
    kv_spec = pl.BlockSpec((n_mem, kv.shape[1]), lambda i: (i // per_seq, 0))
    return pl.pallas_call(
        _mix_mem_kernel,
        out_shape=jax.ShapeDtypeStruct((rows, d), F32),
        grid=(rows // tm,),
        in_specs=[row_block(d), row_block(MIX_WIDTH), row_block(MIX_WIDTH), row_block(d), row_block(d),
                  _resident(w_up_a.shape), _resident(w_up_b.shape), _resident(w_out.shape),
                  _resident((1, d)), _resident(w_q.shape), kv_spec, _resident(w_o.shape)],
        out_specs=row_block(d),
        compiler_params=_params(("parallel",), 56 * MIB),
        name="mix_mem",
    )(x2d, oa, ob, ga, gb, w_up_a, w_up_b, w_out, g_q, w_q, kv, w_o)


def _ffn_kernel(h_ref, g_ref, wg_ref, wu_ref, wd_ref, gf_ref, o_ref, *, final_norm):
    h = h_ref[...]
    n = _rms(h, g_ref[...]).astype(BF16)
    acc = jnp.zeros(h.shape, F32)
    for c in range(wg_ref.shape[0]):
        gate = jnp.dot(n, wg_ref[c], preferred_element_type=F32)
        up = jnp.dot(n, wu_ref[c], preferred_element_type=F32)
        act = (gate / (1.0 + jnp.exp(-gate)) * up).astype(BF16)
        acc = acc + jnp.dot(act, wd_ref[c], preferred_element_type=F32)
    h = h + acc
    o_ref[...] = _rms(h, gf_ref[...]) if final_norm else h


def _ffn(h2d, g, w_gate, w_up, w_down, g_final, final_norm):
    rows, d = h2d.shape
    tm = FFN_ROWS
    row_block = pl.BlockSpec((tm, d), lambda i: (i, 0))
    return pl.pallas_call(
        functools.partial(_ffn_kernel, final_norm=final_norm),
        out_shape=jax.ShapeDtypeStruct((rows, d), F32),
        grid=(rows // tm,),
        in_specs=[row_block, _resident((1, d)), _resident(w_gate.shape), _resident(w_up.shape),
                  _resident(w_down.shape), _resident((1, d))],
        out_specs=row_block,
        compiler_params=_params(("parallel",), 48 * MIB),
        name="ffn",
    )(h2d, g, w_gate, w_up, w_down, g_final)


def kernel(x, mem, positions, g_mix, w_in, w_up_a, w_up_b, w_out, g_mem_q, g_mem_kv,
           w_q_mem, w_kv_mem, w_o_mem, g_ffn, w_ffn_gate, w_ffn_up, w_ffn_down, g_final):
    batch, seq, d = x.shape
    n_mem = mem.shape[1]
    depth = w_in.shape[0]
    d_ff = w_ffn_gate.shape[-1]
    assert seq % ATTN_BLOCK == 0 and seq % MIX_ROWS == 0 and d_ff % FFN_CHUNK == 0
    assert w_in.shape[-1] == 6 * MIX_WIDTH + 2 * d and w_kv_mem.shape[-1] == 2 * MEM_WIDTH

    row = lambda g: g.reshape(1, d)
    h = x.reshape(batch * seq, d)
    pos_col = positions.astype(F32).reshape(batch * seq, 1)
    mem2d = mem.reshape(batch * n_mem, d)

    half = ROPE_DIM // 2
    inv_freq = ROPE_THETA ** (-jnp.arange(half, dtype=F32) / half)
    lane = np.arange(LANES) % HEAD_DIM
    freq = jnp.where(lane < ROPE_DIM, inv_freq[lane % half], 0.0).reshape(1, LANES)
    bias = jnp.asarray(_dilated_bias(seq))
    tri, strict = _sb_constants()
    chunks = d_ff // FFN_CHUNK

    for l in range(depth):
        qa, ka, va, qb, kb, vb, ga, gb = _in_proj(h, pos_col, row(g_mix[l]), w_in[l].astype(BF16), freq)
        dil_scratch = [pltpu.VMEM((seq, 2 * LANES), BF16),
                       pltpu.VMEM((2, seq // ATTN_BLOCK, ATTN_BLOCK, ATTN_BLOCK), F32)]
        oa = _mixer_call(_dil_attn_kernel, "dil_attn", qa, ka, va, (bias,), dil_scratch, batch, seq)
        ob = _sb_attn(qb, kb, vb, tri, strict, batch, seq)
        kv = _mem_kv(mem2d, row(g_mem_kv[l]), w_kv_mem[l].astype(BF16))
        h = _mix_mem(h, oa, ob, ga, gb, w_up_a[l].astype(BF16), w_up_b[l].astype(BF16), w_out[l].astype(BF16),
                     row(g_mem_q[l]), w_q_mem[l].astype(BF16), kv, w_o_mem[l].astype(BF16), seq, n_mem)
        w_gate = w_ffn_gate[l].astype(BF16).reshape(d, chunks, FFN_CHUNK).transpose(1, 0, 2)
        w_up = w_ffn_up[l].astype(BF16).reshape(d, chunks, FFN_CHUNK).transpose(1, 0, 2)
        w_down = w_ffn_down[l].astype(BF16).reshape(chunks, FFN_CHUNK, d)
        h = _ffn(h, row(g_ffn[l]), w_gate, w_up, w_down, row(g_final), final_norm=(l == depth - 1))
    return h.reshape(batch, seq, d)
```

```python
import functools

import numpy as np
import jax
import jax.numpy as jnp
from jax import lax
from jax.experimental import pallas as pl
from jax.experimental.pallas import tpu as pltpu

F32 = jnp.float32
BF16 = jnp.bfloat16

HEAD_DIM = 64
MIX_WIDTH = 512
DIL_PATTERNS = ((128, 1), (512, 4), (2048, 16))
ROPE_THETA = 500000.0
ROPE_DIM = HEAD_DIM // 4
N_HEADS_MEM = 4
MEM_HEAD_DIM = 128
MEM_WIDTH = N_HEADS_MEM * MEM_HEAD_DIM
RMS_EPS = 1e-6

LANES = 128
MIB = 1024 * 1024
MASKED = -1e30
LOG2_E = 1.4426950408889634
SB_DEAD_MASS = 110.0

ATTN_BLOCK = 256
PROJ_ROWS = 1024
PROJ_GROUP = 256
PROJ_CHUNK = 256
MIX_ROWS = 1024
MIX_GROUP = 512
FFN_ROWS = 512
FFN_CHUNK = 256
KV_ROWS = 512


def _rms(x, g):
    y = x * lax.rsqrt(jnp.mean(x * x, axis=-1, keepdims=True) + RMS_EPS)
    return y * g


def _resident(shape):
    zeros = (0,) * len(shape)
    return pl.BlockSpec(shape, lambda *_: zeros, pipeline_mode=pl.Buffered(1))


def _params(semantics, vmem_bytes):
    return pltpu.CompilerParams(dimension_semantics=semantics, vmem_limit_bytes=int(vmem_bytes))


def _mem_kv_kernel(mem_ref, g_ref, w_ref, kv_ref):
    n = _rms(mem_ref[...], g_ref[...]).astype(BF16)
    kv_ref[...] = jnp.dot(n, w_ref[...], preferred_element_type=F32).astype(BF16)


def _mem_kv(mem2d, g, w_kv):
    rows, d = mem2d.shape
    cols = w_kv.shape[1]
    return pl.pallas_call(
        _mem_kv_kernel,
        out_shape=jax.ShapeDtypeStruct((rows, cols), BF16),
        grid=(rows // KV_ROWS,),
        in_specs=[pl.BlockSpec((KV_ROWS, d), lambda i: (i, 0)), _resident((1, d)), _resident((d, cols))],
        out_specs=pl.BlockSpec((KV_ROWS, cols), lambda i: (i, 0)),
        compiler_params=_params(("parallel",), 24 * MIB),
        name="mem_kv",
    )(mem2d, g, w_kv)


def _in_proj_kernel(x_ref, pos_ref, g_ref, w_ref, freq_ref,
                    qa_ref, ka_ref, va_ref, qb_ref, kb_ref, vb_ref, ga_ref, gb_ref):
    half = ROPE_DIM // 2
    rest = HEAD_DIM // half - 2
    scale = HEAD_DIM ** -0.5
    sigmoid = lambda t: 1.0 / (1.0 + jnp.exp(-t))
    w, d = MIX_WIDTH, ga_ref.shape[1]

    for r in range(0, x_ref.shape[0], PROJ_GROUP):
        rows = slice(r, r + PROJ_GROUP)
        n = _rms(x_ref[rows, :], g_ref[...]).astype(BF16)

        ang = freq_ref[...] * pos_ref[0, :, rows]
        cos, sin = jnp.cos(ang), jnp.sin(ang)
        one, zero = jnp.ones_like(ang), jnp.zeros_like(ang)

        def table(x1, x2, other):
            return jnp.concatenate(([x1, x2] + [other] * rest) * (LANES // HEAD_DIM), axis=0).T

        cos_t = table(cos, cos, one)
        sin_lo = table(-sin, zero, zero)
        sin_hi = table(zero, sin, zero)

        def rope(t):
            tiles = []
            for c in range(t.shape[1] // LANES):
                tc = t[:, c * LANES:(c + 1) * LANES]
                tiles.append(tc * cos_t + pltpu.roll(tc, LANES - half, 1) * sin_lo
                             + pltpu.roll(tc, half, 1) * sin_hi)
            return jnp.concatenate(tiles, axis=1)

        outputs = ((qa_ref, w, lambda t: rope(t) * scale), (ka_ref, w, rope), (va_ref, w, None),
                   (qb_ref, w, lambda t: t * scale), (kb_ref, w, None), (vb_ref, w, None),
                   (ga_ref, d, sigmoid), (gb_ref, d, sigmoid))
        col = 0
        for out_ref, width, post in outputs:
            for c in range(0, width, PROJ_CHUNK):
                t = jnp.dot(n, w_ref[:, col + c:col + c + PROJ_CHUNK], preferred_element_type=F32)
                out_ref[rows, c:c + PROJ_CHUNK] = (t if post is None else post(t)).astype(BF16)
            col += width


def _in_proj(x2d, pos_rows, g, w_in, freq):
    rows, d = x2d.shape
    tm = PROJ_ROWS
    row_block = lambda width: pl.BlockSpec((tm, width), lambda i: (i, 0))
    pos_spec = pl.BlockSpec((1, 1, tm), lambda i: (i, 0, 0))
    out_shape = [jax.ShapeDtypeStruct((rows, MIX_WIDTH), BF16)] * 6 + [jax.ShapeDtypeStruct((rows, d), BF16)] * 2
    return pl.pallas_call(
        _in_proj_kernel,
        out_shape=out_shape,
        grid=(rows // tm,),
        in_specs=[row_block(d), pos_spec, _resident((1, d)), _resident(w_in.shape), _resident(freq.shape)],
        out_specs=[row_block(MIX_WIDTH)] * 6 + [row_block(d)] * 2,
        compiler_params=_params(("parallel",), 56 * MIB),
        name="in_proj",
    )(x2d, pos_rows, g, w_in, freq)


def _split_heads(q):
    first = lax.broadcasted_iota(jnp.int32, (1, LANES), 1) < HEAD_DIM
    zero = jnp.zeros_like(q)
    return first, (jnp.where(first, q, zero), jnp.where(first, zero, q))


def _scores(qh, kj):
    return lax.dot_general(qh, kj, (((1,), (1,)), ((), ())), preferred_element_type=F32)


def _dil_attn_kernel(q_ref, k_ref, v_ref, bias_ref, o_ref, vones_ref, s_ref):
    blk = ATTN_BLOCK
    nq = k_ref.shape[0] // blk
    tile = lambda i, j: i * (i + 1) // 2 + j

    vones_ref[:, :LANES] = v_ref[...]
    vones_ref[:, LANES:] = jnp.ones((k_ref.shape[0], LANES), BF16)

    heads = [_split_heads(q_ref[i * blk:(i + 1) * blk, :]) for i in range(nq)]
    row_max = {}
    for j in range(nq):
        kj = k_ref[j * blk:(j + 1) * blk, :]
        for i in range(j, nq):
            bias = bias_ref[i - j]
            for h in range(2):
                s = _scores(heads[i][1][h], kj) + bias
                s_ref[h, tile(i, j)] = s
                part = jnp.maximum(s[:, :LANES], s[:, LANES:])
                row_max[i, h] = part if j == 0 else jnp.maximum(row_max[i, h], part)
    for i in range(nq):
        outs = []
        for h in range(2):
            m = jnp.broadcast_to(jnp.max(row_max[i, h], axis=-1, keepdims=True), (blk, LANES))
            m = jnp.concatenate([m] * (blk // LANES), axis=1)
            acc = None
            for j in range(i + 1):
                p = jnp.exp(s_ref[h, tile(i, j)] - m).astype(BF16)
                pv = jnp.dot(p, vones_ref[j * blk:(j + 1) * blk, :], preferred_element_type=F32)
                acc = pv if acc is None else acc + pv
            outs.append(acc[:, :LANES] / acc[:, LANES:])
        o_ref[i * blk:(i + 1) * blk, :] = jnp.where(heads[i][0], outs[0], outs[1]).astype(BF16)


def _sb_attn_kernel(q_ref, k_ref, v_ref, tri_ref, mask_ref, o_ref,
                    c_ref, acc_ref, hi_ref, lo_ref, ls_ref, a_ref):
    blk = ATTN_BLOCK
    nq = q_ref.shape[0] // blk
    tri = tri_ref[...]

    def query_heads(i):
        return _split_heads(q_ref[i * blk:(i + 1) * blk, :])[1]

    def softplus_pass(i, h, qh, kj, diagonal):
        z = _scores(qh, kj)
        if diagonal:
            z = z + mask_ref[...]
        sp = jnp.maximum(z, 0.0) + jnp.log(1.0 + jnp.exp2(jnp.abs(z) * -LOG2_E))
        ls_ref[i, h] = z - sp
        hi = sp.astype(BF16)
        hi_ref[i, h] = hi
        lo_ref[i, h] = (sp - hi.astype(F32)).astype(BF16)
        return sp[:, :1]

    def weight_pass(i, h, first_col, c, diagonal):
        within = (jnp.dot(hi_ref[i, h], tri, preferred_element_type=F32)
                  + jnp.dot(lo_ref[i, h], tri, preferred_element_type=F32))
        logit = ls_ref[i, h] - within
        if c is not None:
            logit = logit - jnp.concatenate([c] * (blk // LANES), axis=1)
        a_ref[i, h] = jnp.exp(logit).astype(BF16)
        return jnp.broadcast_to(within[:, :1] + first_col, (blk, LANES))

    def diagonal_sweep(rows, key_start, d):
        first_col, total = {}, {}
        for i in rows:
            qh = query_heads(i)
            kj = k_ref[pl.ds(key_start(i), blk), :]
            for h in range(2):
                first_col[i, h] = softplus_pass(i, h, qh[h], kj, d is None)
        for i in rows:
            for h in range(2):
                c = None if d is None else c_ref[i, h]
                total[i, h] = weight_pass(i, h, first_col[i, h], c, d is None)
        pending_min = None
        for i in rows:
            vj = v_ref[pl.ds(key_start(i), blk), :]
            for h in range(2):
                pv = jnp.dot(a_ref[i, h], vj, preferred_element_type=F32)
                if d is None:
                    c_ref[i, h] = total[i, h]
                    acc_ref[i, h] = pv
                else:
                    live = (i >= d).astype(F32)
                    c = c_ref[i, h] + live * total[i, h]
                    c_ref[i, h] = c
                    acc_ref[i, h] += live * pv
                    pending = c + (i <= d).astype(F32) * (2.0 * SB_DEAD_MASS)
                    pending_min = pending if pending_min is None else jnp.minimum(pending_min, pending)
        return pending_min

    diagonal_sweep(range(nq), lambda i: i * blk, None)

    def body(state):
        d, _ = state
        start = lambda i: pl.multiple_of(jnp.maximum(i - d, 0) * blk, blk)
        pending_min = diagonal_sweep(range(1, nq), start, d)
        return d + 1, (d + 1 < nq) & (jnp.min(pending_min) <= SB_DEAD_MASS)

    lax.while_loop(lambda state: state[1], body, (jnp.int32(1), jnp.asarray(nq > 1)))
    first = lax.broadcasted_iota(jnp.int32, (1, LANES), 1) < HEAD_DIM
    for i in range(nq):
        o_ref[i * blk:(i + 1) * blk, :] = jnp.where(first, acc_ref[i, 0], acc_ref[i, 1]).astype(BF16)


def _sb_attn(q, k, v, tri, strict, batch, seq):
    nq = seq // ATTN_BLOCK
    spec = pl.BlockSpec((seq, LANES), lambda b, p: (b, p))
    state = pltpu.VMEM((nq, 2, ATTN_BLOCK, LANES), F32)
    tiles = lambda dtype: pltpu.VMEM((nq, 2, ATTN_BLOCK, ATTN_BLOCK), dtype)
    return pl.pallas_call(
        _sb_attn_kernel,
        out_shape=jax.ShapeDtypeStruct(q.shape, BF16),
        grid=(batch, MIX_WIDTH // LANES),
        in_specs=[spec, spec, spec, _resident(tri.shape), _resident(strict.shape)],
        out_specs=spec,
        scratch_shapes=[state, state, tiles(BF16), tiles(BF16), tiles(F32), tiles(BF16)],
        compiler_params=_params(("parallel", "parallel"), 48 * MIB),
        name="sb_attn",
    )(q, k, v, tri, strict)


def _dil_attn(q, k, v, bias, batch, seq):
    nq = seq // ATTN_BLOCK
    spec = pl.BlockSpec((seq, LANES), lambda b, p: (b, p))
    scratch = [pltpu.VMEM((seq, 2 * LANES), BF16),
               pltpu.VMEM((2, nq * (nq + 1) // 2, ATTN_BLOCK, ATTN_BLOCK), F32)]
    return pl.pallas_call(
        _dil_attn_kernel,
        out_shape=jax.ShapeDtypeStruct(q.shape, BF16),
        grid=(batch, MIX_WIDTH // LANES),
        in_specs=[spec, spec, spec, _resident(bias.shape)],
        out_specs=spec,
        scratch_shapes=scratch,
        compiler_params=_params(("parallel", "parallel"), 48 * MIB),
        name="dil_attn",
    )(q, k, v, bias)


def _dilated_bias(seq):
    blk = ATTN_BLOCK
    kd = np.arange(seq // blk)[:, None, None]
    dist = kd * blk + np.arange(blk)[None, :, None] - np.arange(blk)[None, None, :]
    count = np.zeros(dist.shape, np.float64)
    for window, dilation in DIL_PATTERNS:
        count += (dist >= 0) & (dist <= window) & (dist % dilation == 0)
    return np.where(count > 0, np.log(np.maximum(count, 1.0)), MASKED).astype(np.float32)


def _sb_constants():
    blk = ATTN_BLOCK
    tri = np.arange(blk)[:, None] > np.arange(blk)[None, :]
    strict = np.arange(blk)[None, :] < np.arange(blk)[:, None]
    return jnp.asarray(tri, BF16), jnp.asarray(np.where(strict, 0.0, MASKED), F32)


def _mix_mem_kernel(x_ref, oa_ref, ob_ref, ga_ref, gb_ref, wua_ref, wub_ref, wout_ref,
                    gq_ref, wq_ref, kv_ref, wo_ref, h_ref):
    for r in range(0, x_ref.shape[0], MIX_GROUP):
        rows = slice(r, r + MIX_GROUP)
        ua = jnp.dot(oa_ref[rows, :], wua_ref[...], preferred_element_type=F32)
        ub = jnp.dot(ob_ref[rows, :], wub_ref[...], preferred_element_type=F32)
        mixed = ga_ref[rows, :].astype(F32) * ua + gb_ref[rows, :].astype(F32) * ub
        h = x_ref[rows, :] + jnp.dot(mixed.astype(BF16), wout_ref[...], preferred_element_type=F32)

        hn = _rms(h, gq_ref[...]).astype(BF16)
        q = jnp.dot(hn, wq_ref[...], preferred_element_type=F32).astype(BF16)
        heads = []
        for hd in range(N_HEADS_MEM):
            lo, hi = hd * MEM_HEAD_DIM, (hd + 1) * MEM_HEAD_DIM
            s = _scores(q[:, lo:hi], kv_ref[:, lo:hi]) * (MEM_HEAD_DIM ** -0.5)
            p = jnp.exp(s - jnp.max(s, axis=-1, keepdims=True))
            o = jnp.dot(p.astype(BF16), kv_ref[:, MEM_WIDTH + lo:MEM_WIDTH + hi], preferred_element_type=F32)
            heads.append(o / jnp.sum(p, axis=-1, keepdims=True))
        o = jnp.concatenate(heads, axis=1).astype(BF16)
        h_ref[rows, :] = h + jnp.dot(o, wo_ref[...], preferred_element_type=F32)


def _mix_mem(x2d, oa, ob, ga, gb, w_up_a, w_up_b, w_out, g_q, w_q, kv, w_o, seq, n_mem):
    rows, d = x2d.shape
    tm = MIX_ROWS
    per_seq = seq // tm
    row_block = lambda width: pl.BlockSpec((tm, width), lambda i: (i, 0))
    kv_spec = pl.BlockSpec((n_mem, kv.shape[1]), lambda i: (i // per_seq, 0))
    return pl.pallas_call(
        _mix_mem_kernel,
        out_shape=jax.ShapeDtypeStruct((rows, d), F32),
        grid=(rows // tm,),
        in_specs=[row_block(d), row_block(MIX_WIDTH), row_block(MIX_WIDTH), row_block(d), row_block(d),
                  _resident(w_up_a.shape), _resident(w_up_b.shape), _resident(w_out.shape),
                  _resident((1, d)), _resident(w_q.shape), kv_spec, _resident(w_o.shape)],
        out_specs=row_block(d),
        compiler_params=_params(("parallel",), 56 * MIB),
        name="mix_mem",
    )(x2d, oa, ob, ga, gb, w_up_a, w_up_b, w_out, g_q, w_q, kv, w_o)


def _ffn_kernel(h_ref, g_ref, wg_ref, wu_ref, wd_ref, gf_ref, o_ref, *, final_norm):
    h = h_ref[...]
    n = _rms(h, g_ref[...]).astype(BF16)
    acc = jnp.zeros(h.shape, F32)
    for c in range(wg_ref.shape[0]):
        gate = jnp.dot(n, wg_ref[c], preferred_element_type=F32)
        up = jnp.dot(n, wu_ref[c], preferred_element_type=F32)
        act = (gate / (1.0 + jnp.exp(-gate)) * up).astype(BF16)
        acc = acc + jnp.dot(act, wd_ref[c], preferred_element_type=F32)
    h = h + acc
    o_ref[...] = _rms(h, gf_ref[...]) if final_norm else h


def _ffn(h2d, g, w_gate, w_up, w_down, g_final, final_norm):
    rows, d = h2d.shape
    tm = FFN_ROWS
    row_block = pl.BlockSpec((tm, d), lambda i: (i, 0))
    return pl.pallas_call(
        functools.partial(_ffn_kernel, final_norm=final_norm),
        out_shape=jax.ShapeDtypeStruct((rows, d), F32),
        grid=(rows // tm,),
        in_specs=[row_block, _resident((1, d)), _resident(w_gate.shape), _resident(w_up.shape),
                  _resident(w_down.shape), _resident((1, d))],
        out_specs=row_block,
        compiler_params=_params(("parallel",), 48 * MIB),
        name="ffn",
    )(h2d, g, w_gate, w_up, w_down, g_final)


def kernel(x, mem, positions, g_mix, w_in, w_up_a, w_up_b, w_out, g_mem_q, g_mem_kv,
           w_q_mem, w_kv_mem, w_o_mem, g_ffn, w_ffn_gate, w_ffn_up, w_ffn_down, g_final):
    batch, seq, d = x.shape
    n_mem = mem.shape[1]
    depth = w_in.shape[0]
    d_ff = w_ffn_gate.shape[-1]
    assert seq % ATTN_BLOCK == 0 and seq % MIX_ROWS == 0 and d_ff % FFN_CHUNK == 0
    assert w_in.shape[-1] == 6 * MIX_WIDTH + 2 * d and w_kv_mem.shape[-1] == 2 * MEM_WIDTH

    row = lambda g: g.reshape(1, d)
    h = x.reshape(batch * seq, d)
    pos_rows = positions.astype(F32).reshape(batch * seq // PROJ_ROWS, 1, PROJ_ROWS)
    mem2d = mem.reshape(batch * n_mem, d)

    half = ROPE_DIM // 2
    freq = (ROPE_THETA ** (-jnp.arange(half, dtype=F32) / half)).reshape(half, 1)
    bias = jnp.asarray(_dilated_bias(seq))
    tri, strict = _sb_constants()
    chunks = d_ff // FFN_CHUNK

    for l in range(depth):
        qa, ka, va, qb, kb, vb, ga, gb = _in_proj(h, pos_rows, row(g_mix[l]), w_in[l].astype(BF16), freq)
        oa = _dil_attn(qa, ka, va, bias, batch, seq)
        ob = _sb_attn(qb, kb, vb, tri, strict, batch, seq)
        kv = _mem_kv(mem2d, row(g_mem_kv[l]), w_kv_mem[l].astype(BF16))
        h = _mix_mem(h, oa, ob, ga, gb, w_up_a[l].astype(BF16), w_up_b[l].astype(BF16), w_out[l].astype(BF16),
                     row(g_mem_q[l]), w_q_mem[l].astype(BF16), kv, w_o_mem[l].astype(BF16), seq, n_mem)
        w_gate = w_ffn_gate[l].astype(BF16).reshape(d, chunks, FFN_CHUNK).transpose(1, 0, 2)
        w_up = w_ffn_up[l].astype(BF16).reshape(d, chunks, FFN_CHUNK).transpose(1, 0, 2)
        w_down = w_ffn_down[l].astype(BF16).reshape(chunks, FFN_CHUNK, d)
        h = _ffn(h, row(g_ffn[l]), w_gate, w_up, w_down, row(g_final), final_norm=(l == depth - 1))
    return h.reshape(batch, seq, d)
```

```python
import functools

import numpy as np
import jax
import jax.numpy as jnp
from jax import lax
from jax.experimental import pallas as pl
from jax.experimental.pallas import tpu as pltpu

F32 = jnp.float32
BF16 = jnp.bfloat16

HEAD_DIM = 64
MIX_WIDTH = 512
DIL_PATTERNS = ((128, 1), (512, 4), (2048, 16))
ROPE_THETA = 500000.0
ROPE_DIM = HEAD_DIM // 4
N_HEADS_MEM = 4
MEM_HEAD_DIM = 128
MEM_WIDTH = N_HEADS_MEM * MEM_HEAD_DIM
RMS_EPS = 1e-6

LANES = 128
MIB = 1024 * 1024
MASKED = -1e30
LOG2_E = 1.4426950408889634
SB_DEAD_MASS = 110.0

ATTN_BLOCK = 256
PROJ_ROWS = 1024
PROJ_GROUP = 256
PROJ_CHUNK = 256
MIX_ROWS = 1024
MIX_GROUP = 512
FFN_ROWS = 1024
FFN_GROUP = 512
FFN_CHUNK = 256
KV_ROWS = 512


def _rms(x, g):
    y = x * lax.rsqrt(jnp.mean(x * x, axis=-1, keepdims=True) + RMS_EPS)
    return y * g


def _resident(shape):
    zeros = (0,) * len(shape)
    return pl.BlockSpec(shape, lambda *_: zeros, pipeline_mode=pl.Buffered(1))


def _params(semantics, vmem_bytes):
    return pltpu.CompilerParams(dimension_semantics=semantics, vmem_limit_bytes=int(vmem_bytes))


def _mem_kv_kernel(mem_ref, g_ref, w_ref, kv_ref):
    n = _rms(mem_ref[...], g_ref[...]).astype(BF16)
    kv_ref[...] = jnp.dot(n, w_ref[...], preferred_element_type=F32).astype(BF16)


def _mem_kv(mem2d, g, w_kv):
    rows, d = mem2d.shape
    cols = w_kv.shape[1]
    return pl.pallas_call(
        _mem_kv_kernel,
        out_shape=jax.ShapeDtypeStruct((rows, cols), BF16),
        grid=(rows // KV_ROWS,),
        in_specs=[pl.BlockSpec((KV_ROWS, d), lambda i: (i, 0)), _resident((1, d)), _resident((d, cols))],
        out_specs=pl.BlockSpec((KV_ROWS, cols), lambda i: (i, 0)),
        compiler_params=_params(("parallel",), 24 * MIB),
        name="mem_kv",
    )(mem2d, g, w_kv)


def _in_proj_kernel(x_ref, pos_ref, g_ref, w_ref, freq_ref,
                    qa_ref, ka_ref, va_ref, qb_ref, kb_ref, vb_ref, ga_ref, gb_ref):
    half = ROPE_DIM // 2
    rest = HEAD_DIM // half - 2
    scale = HEAD_DIM ** -0.5
    sigmoid = lambda t: 1.0 / (1.0 + jnp.exp(-t))
    w, d = MIX_WIDTH, ga_ref.shape[1]

    for r in range(0, x_ref.shape[0], PROJ_GROUP):
        rows = slice(r, r + PROJ_GROUP)
        n = _rms(x_ref[rows, :], g_ref[...]).astype(BF16)

        ang = freq_ref[...] * pos_ref[0, :, rows]
        cos, sin = jnp.cos(ang), jnp.sin(ang)
        one, zero = jnp.ones_like(ang), jnp.zeros_like(ang)

        def table(x1, x2, other):
            return jnp.concatenate(([x1, x2] + [other] * rest) * (LANES // HEAD_DIM), axis=0).T

        cos_t = table(cos, cos, one)
        sin_lo = table(-sin, zero, zero)
        sin_hi = table(zero, sin, zero)

        def rope(t):
            tiles = []
            for c in range(t.shape[1] // LANES):
                tc = t[:, c * LANES:(c + 1) * LANES]
                tiles.append(tc * cos_t + pltpu.roll(tc, LANES - half, 1) * sin_lo
                             + pltpu.roll(tc, half, 1) * sin_hi)
            return jnp.concatenate(tiles, axis=1)

        outputs = ((qa_ref, w, lambda t: rope(t) * scale), (ka_ref, w, rope), (va_ref, w, None),
                   (qb_ref, w, lambda t: t * scale), (kb_ref, w, None), (vb_ref, w, None),
                   (ga_ref, d, sigmoid), (gb_ref, d, sigmoid))
        col = 0
        for out_ref, width, post in outputs:
            for c in range(0, width, PROJ_CHUNK):
                t = jnp.dot(n, w_ref[:, col + c:col + c + PROJ_CHUNK], preferred_element_type=F32)
                out_ref[rows, c:c + PROJ_CHUNK] = (t if post is None else post(t)).astype(BF16)
            col += width


def _in_proj(x2d, pos_rows, g, w_in, freq):
    rows, d = x2d.shape
    tm = PROJ_ROWS
    row_block = lambda width: pl.BlockSpec((tm, width), lambda i: (i, 0))
    pos_spec = pl.BlockSpec((1, 1, tm), lambda i: (i, 0, 0))
    out_shape = [jax.ShapeDtypeStruct((rows, MIX_WIDTH), BF16)] * 6 + [jax.ShapeDtypeStruct((rows, d), BF16)] * 2
    return pl.pallas_call(
        _in_proj_kernel,
        out_shape=out_shape,
        grid=(rows // tm,),
        in_specs=[row_block(d), pos_spec, _resident((1, d)), _resident(w_in.shape), _resident(freq.shape)],
        out_specs=[row_block(MIX_WIDTH)] * 6 + [row_block(d)] * 2,
        compiler_params=_params(("parallel",), 56 * MIB),
        name="in_proj",
    )(x2d, pos_rows, g, w_in, freq)


def _split_heads(q):
    first = lax.broadcasted_iota(jnp.int32, (1, LANES), 1) < HEAD_DIM
    zero = jnp.zeros_like(q)
    return first, (jnp.where(first, q, zero), jnp.where(first, zero, q))


def _scores(qh, kj):
    return lax.dot_general(qh, kj, (((1,), (1,)), ((), ())), preferred_element_type=F32)


def _dil_attn_kernel(q_ref, k_ref, v_ref, bias_ref, o_ref, vones_ref, s_ref):
    blk = ATTN_BLOCK
    nq = k_ref.shape[0] // blk
    tile = lambda i, j: i * (i + 1) // 2 + j

    vones_ref[:, :LANES] = v_ref[...]
    vones_ref[:, LANES:] = jnp.ones((k_ref.shape[0], LANES), BF16)

    heads = [_split_heads(q_ref[i * blk:(i + 1) * blk, :]) for i in range(nq)]
    row_max = {}
    for j in range(nq):
        kj = k_ref[j * blk:(j + 1) * blk, :]
        for i in range(j, nq):
            bias = bias_ref[i - j]
            for h in range(2):
                s = _scores(heads[i][1][h], kj) + bias
                s_ref[h, tile(i, j)] = s
                part = jnp.maximum(s[:, :LANES], s[:, LANES:])
                row_max[i, h] = part if j == 0 else jnp.maximum(row_max[i, h], part)
    for i in range(nq):
        outs = []
        for h in range(2):
            m = jnp.broadcast_to(jnp.max(row_max[i, h], axis=-1, keepdims=True), (blk, LANES))
            m = jnp.concatenate([m] * (blk // LANES), axis=1)
            acc = None
            for j in range(i + 1):
                p = jnp.exp(s_ref[h, tile(i, j)] - m).astype(BF16)
                pv = jnp.dot(p, vones_ref[j * blk:(j + 1) * blk, :], preferred_element_type=F32)
                acc = pv if acc is None else acc + pv
            outs.append(acc[:, :LANES] / acc[:, LANES:])
        o_ref[i * blk:(i + 1) * blk, :] = jnp.where(heads[i][0], outs[0], outs[1]).astype(BF16)


def _sb_attn_kernel(q_ref, k_ref, v_ref, tri_ref, mask_ref, o_ref,
                    c_ref, acc_ref, sp_ref, ls_ref, a_ref):
    blk = ATTN_BLOCK
    nq = q_ref.shape[0] // blk
    tri = tri_ref[...]

    def query_heads(i):
        return _split_heads(q_ref[i * blk:(i + 1) * blk, :])[1]

    full_tile = ((slice(0, blk), blk),)
    diagonal_tile = ((slice(0, blk // 2), blk // 2), (slice(blk // 2, blk), blk))

    def softplus_pass(i, h, part, qh, kj, diagonal):
        rows, keys = part
        z = _scores(qh[rows], kj[:keys])
        if diagonal:
            z = z + mask_ref[rows, :keys]
        sp = jnp.maximum(z, 0.0) + jnp.log(1.0 + jnp.exp2(jnp.abs(z) * -LOG2_E))
        ls_ref[i, h, rows, :keys] = z - sp
        sp_ref[i, h, rows, :keys] = sp.astype(BF16)
        return sp[:, :1]

    def weight_pass(i, h, part, first_col, c):
        rows, keys = part
        later = tri[:keys, :keys]
        within = jnp.dot(sp_ref[i, h, rows, :keys], later, preferred_element_type=F32)
        logit = ls_ref[i, h, rows, :keys] - within
        if c is not None:
            logit = logit - jnp.concatenate([c] * (keys // LANES), axis=1)
        a_ref[i, h, rows, :keys] = jnp.exp(logit).astype(BF16)
        return jnp.broadcast_to(within[:, :1] + first_col, (within.shape[0], LANES))

    def diagonal_sweep(blocks, key_start, d):
        parts = diagonal_tile if d is None else full_tile
        tiles = [(i, h, part) for i in blocks for h in range(2) for part in parts]
        first_col, total = {}, {}
        for i in blocks:
            qh = query_heads(i)
            kj = k_ref[pl.ds(key_start(i), blk), :]
            for h in range(2):
                for part in parts:
                    first_col[i, h, part[1]] = softplus_pass(i, h, part, qh[h], kj, d is None)
        for i, h, part in tiles:
            c = None if d is None else c_ref[i, h]
            total[i, h, part[1]] = weight_pass(i, h, part, first_col[i, h, part[1]], c)
        pending_min = None
        for i in blocks:
            vj = v_ref[pl.ds(key_start(i), blk), :]
            for h in range(2):
                for rows, keys in parts:
                    pv = jnp.dot(a_ref[i, h, rows, :keys], vj[:keys], preferred_element_type=F32)
                    if d is None:
                        c_ref[i, h, rows, :] = total[i, h, keys]
                        acc_ref[i, h, rows, :] = pv
                    else:
                        live = (i >= d).astype(F32)
                        c = c_ref[i, h] + live * total[i, h, keys]
                        c_ref[i, h] = c
                        acc_ref[i, h] += live * pv
                        pending = c + (i <= d).astype(F32) * (2.0 * SB_DEAD_MASS)
                        pending_min = pending if pending_min is None else jnp.minimum(pending_min, pending)
        return pending_min

    diagonal_sweep(range(nq), lambda i: i * blk, None)

    def body(state):
        d, _ = state
        start = lambda i: pl.multiple_of(jnp.maximum(i - d, 0) * blk, blk)
        pending_min = diagonal_sweep(range(1, nq), start, d)
        return d + 1, (d + 1 < nq) & (jnp.min(pending_min) <= SB_DEAD_MASS)

    lax.while_loop(lambda state: state[1], body, (jnp.int32(1), jnp.asarray(nq > 1)))
    first = lax.broadcasted_iota(jnp.int32, (1, LANES), 1) < HEAD_DIM
    for i in range(nq):
        o_ref[i * blk:(i + 1) * blk, :] = jnp.where(first, acc_ref[i, 0], acc_ref[i, 1]).astype(BF16)


def _sb_attn(q, k, v, tri, strict, batch, seq):
    nq = seq // ATTN_BLOCK
    spec = pl.BlockSpec((seq, LANES), lambda b, p: (b, p))
    state = pltpu.VMEM((nq, 2, ATTN_BLOCK, LANES), F32)
    tiles = lambda dtype: pltpu.VMEM((nq, 2, ATTN_BLOCK, ATTN_BLOCK), dtype)
    return pl.pallas_call(
        _sb_attn_kernel,
        out_shape=jax.ShapeDtypeStruct(q.shape, BF16),
        grid=(batch, MIX_WIDTH // LANES),
        in_specs=[spec, spec, spec, _resident(tri.shape), _resident(strict.shape)],
        out_specs=spec,
        scratch_shapes=[state, state, tiles(BF16), tiles(F32), tiles(BF16)],
        compiler_params=_params(("parallel", "parallel"), 48 * MIB),
        name="sb_attn",
    )(q, k, v, tri, strict)


def _dil_attn(q, k, v, bias, batch, seq):
    nq = seq // ATTN_BLOCK
    spec = pl.BlockSpec((seq, LANES), lambda b, p: (b, p))
    scratch = [pltpu.VMEM((seq, 2 * LANES), BF16),
               pltpu.VMEM((2, nq * (nq + 1) // 2, ATTN_BLOCK, ATTN_BLOCK), F32)]
    return pl.pallas_call(
        _dil_attn_kernel,
        out_shape=jax.ShapeDtypeStruct(q.shape, BF16),
        grid=(batch, MIX_WIDTH // LANES),
        in_specs=[spec, spec, spec, _resident(bias.shape)],
        out_specs=spec,
        scratch_shapes=scratch,
        compiler_params=_params(("parallel", "parallel"), 48 * MIB),
        name="dil_attn",
    )(q, k, v, bias)


def _dilated_bias(seq):
    blk = ATTN_BLOCK
    kd = np.arange(seq // blk)[:, None, None]
    dist = kd * blk + np.arange(blk)[None, :, None] - np.arange(blk)[None, None, :]
    count = np.zeros(dist.shape, np.float64)
    for window, dilation in DIL_PATTERNS:
        count += (dist >= 0) & (dist <= window) & (dist % dilation == 0)
    return np.where(count > 0, np.log(np.maximum(count, 1.0)), MASKED).astype(np.float32)


def _sb_constants():
    blk = ATTN_BLOCK
    tri = np.arange(blk)[:, None] > np.arange(blk)[None, :]
    strict = np.arange(blk)[None, :] < np.arange(blk)[:, None]
    return jnp.asarray(tri, BF16), jnp.asarray(np.where(strict, 0.0, MASKED), F32)


def _mix_mem_kernel(x_ref, oa_ref, ob_ref, ga_ref, gb_ref, wua_ref, wub_ref, wout_ref,
                    gq_ref, wq_ref, kv_ref, wo_ref, h_ref):
    for r in range(0, x_ref.shape[0], MIX_GROUP):
        rows = slice(r, r + MIX_GROUP)
        ua = jnp.dot(oa_ref[rows, :], wua_ref[...], preferred_element_type=F32)
        ub = jnp.dot(ob_ref[rows, :], wub_ref[...], preferred_element_type=F32)
        mixed = ga_ref[rows, :].astype(F32) * ua + gb_ref[rows, :].astype(F32) * ub
        h = x_ref[rows, :] + jnp.dot(mixed.astype(BF16), wout_ref[...], preferred_element_type=F32)

        hn = _rms(h, gq_ref[...]).astype(BF16)
        q = jnp.dot(hn, wq_ref[...], preferred_element_type=F32).astype(BF16)
        heads = []
        for hd in range(N_HEADS_MEM):
            lo, hi = hd * MEM_HEAD_DIM, (hd + 1) * MEM_HEAD_DIM
            s = _scores(q[:, lo:hi], kv_ref[:, lo:hi]) * (MEM_HEAD_DIM ** -0.5)
            p = jnp.exp(s - jnp.max(s, axis=-1, keepdims=True))
            o = jnp.dot(p.astype(BF16), kv_ref[:, MEM_WIDTH + lo:MEM_WIDTH + hi], preferred_element_type=F32)
            heads.append(o / jnp.sum(p, axis=-1, keepdims=True))
        o = jnp.concatenate(heads, axis=1).astype(BF16)
        h_ref[rows, :] = h + jnp.dot(o, wo_ref[...], preferred_element_type=F32)


def _mix_mem(x2d, oa, ob, ga, gb, w_up_a, w_up_b, w_out, g_q, w_q, kv, w_o, seq, n_mem):
    rows, d = x2d.shape
    tm = MIX_ROWS
    per_seq = seq // tm
    row_block = lambda width: pl.BlockSpec((tm, width), lambda i: (i, 0))
    kv_spec = pl.BlockSpec((n_mem, kv.shape[1]), lambda i: (i // per_seq, 0))
    return pl.pallas_call(
        _mix_mem_kernel,
        out_shape=jax.ShapeDtypeStruct((rows, d), F32),
        grid=(rows // tm,),
        in_specs=[row_block(d), row_block(MIX_WIDTH), row_block(MIX_WIDTH), row_block(d), row_block(d),
                  _resident(w_up_a.shape), _resident(w_up_b.shape), _resident(w_out.shape),
                  _resident((1, d)), _resident(w_q.shape), kv_spec, _resident(w_o.shape)],
        out_specs=row_block(d),
        compiler_params=_params(("parallel",), 56 * MIB),
        name="mix_mem",
    )(x2d, oa, ob, ga, gb, w_up_a, w_up_b, w_out, g_q, w_q, kv, w_o)


def _ffn_kernel(h_ref, g_ref, wg_ref, wu_ref, wd_ref, gf_ref, o_ref, *, final_norm):
    for r in range(0, h_ref.shape[0], FFN_GROUP):
        rows = slice(r, r + FFN_GROUP)
        h = h_ref[rows, :]
        n = _rms(h, g_ref[...]).astype(BF16)
        acc = jnp.zeros(h.shape, F32)
        for c in range(wg_ref.shape[0]):
            gate = jnp.dot(n, wg_ref[c], preferred_element_type=F32)
            up = jnp.dot(n, wu_ref[c], preferred_element_type=F32)
            act = (gate / (1.0 + jnp.exp(-gate)) * up).astype(BF16)
            acc = acc + jnp.dot(act, wd_ref[c], preferred_element_type=F32)
        h = h + acc
        o_ref[rows, :] = _rms(h, gf_ref[...]) if final_norm else h


def _ffn(h2d, g, w_gate, w_up, w_down, g_final, final_norm):
    rows, d = h2d.shape
    tm = FFN_ROWS
    row_block = pl.BlockSpec((tm, d), lambda i: (i, 0))
    return pl.pallas_call(
        functools.partial(_ffn_kernel, final_norm=final_norm),
        out_shape=jax.ShapeDtypeStruct((rows, d), F32),
        grid=(rows // tm,),
        in_specs=[row_block, _resident((1, d)), _resident(w_gate.shape), _resident(w_up.shape),
                  _resident(w_down.shape), _resident((1, d))],
        out_specs=row_block,
        compiler_params=_params(("parallel",), 48 * MIB),
        name="ffn",
    )(h2d, g, w_gate, w_up, w_down, g_final)


def kernel(x, mem, positions, g_mix, w_in, w_up_a, w_up_b, w_out, g_mem_q, g_mem_kv,
           w_q_mem, w_kv_mem, w_o_mem, g_ffn, w_ffn_gate, w_ffn_up, w_ffn_down, g_final):
    batch, seq, d = x.shape
    n_mem = mem.shape[1]
    depth = w_in.shape[0]
    d_ff = w_ffn_gate.shape[-1]
    assert seq % ATTN_BLOCK == 0 and seq % MIX_ROWS == 0 and d_ff % FFN_CHUNK == 0
    assert w_in.shape[-1] == 6 * MIX_WIDTH + 2 * d and w_kv_mem.shape[-1] == 2 * MEM_WIDTH

    row = lambda g: g.reshape(1, d)
    h = x.reshape(batch * seq, d)
    pos_rows = positions.astype(F32).reshape(batch * seq // PROJ_ROWS, 1, PROJ_ROWS)
    mem2d = mem.reshape(batch * n_mem, d)

    half = ROPE_DIM // 2
    freq = (ROPE_THETA ** (-jnp.arange(half, dtype=F32) / half)).reshape(half, 1)
    bias = jnp.asarray(_dilated_bias(seq))
    tri, strict = _sb_constants()
    chunks = d_ff // FFN_CHUNK

    for l in range(depth):
        qa, ka, va, qb, kb, vb, ga, gb = _in_proj(h, pos_rows, row(g_mix[l]), w_in[l].astype(BF16), freq)
        oa = _dil_attn(qa, ka, va, bias, batch, seq)
        ob = _sb_attn(qb, kb, vb, tri, strict, batch, seq)
        kv = _mem_kv(mem2d, row(g_mem_kv[l]), w_kv_mem[l].astype(BF16))
        h = _mix_mem(h, oa, ob, ga, gb, w_up_a[l].astype(BF16), w_up_b[l].astype(BF16), w_out[l].astype(BF16),
                     row(g_mem_q[l]), w_q_mem[l].astype(BF16), kv, w_o_mem[l].astype(BF16), seq, n_mem)
        w_gate = w_ffn_gate[l].astype(BF16).reshape(d, chunks, FFN_CHUNK).transpose(1, 0, 2)
        w_up = w_ffn_up[l].astype(BF16).reshape(d, chunks, FFN_CHUNK).transpose(1, 0, 2)
        w_down = w_ffn_down[l].astype(BF16).reshape(chunks, FFN_CHUNK, d)
        h = _ffn(h, row(g_ffn[l]), w_gate, w_up, w_down, row(g_final), final_norm=(l == depth - 1))
    return h.reshape(batch, seq, d)
```

```python
import functools

import numpy as np
import jax
import jax.numpy as jnp
from jax import lax
from jax.experimental import pallas as pl
from jax.experimental.pallas import tpu as pltpu

F32 = jnp.float32
BF16 = jnp.bfloat16

HEAD_DIM = 64
MIX_WIDTH = 512
DIL_PATTERNS = ((128, 1), (512, 4), (2048, 16))
ROPE_THETA = 500000.0
ROPE_DIM = HEAD_DIM // 4
N_HEADS_MEM = 4
MEM_HEAD_DIM = 128
MEM_WIDTH = N_HEADS_MEM * MEM_HEAD_DIM
RMS_EPS = 1e-6

LANES = 128
MIB = 1024 * 1024
MASKED = -1e30
LOG2_E = 1.4426950408889634
SB_DEAD_MASS = 110.0

ATTN_BLOCK = 256
A_SPLIT = 6
PROJ_ROWS = 1024
PROJ_GROUP = 256
PROJ_CHUNK = 256
MIX_ROWS = 1024
MIX_GROUP = 512
FFN_ROWS = 1024
FFN_GROUP = 512
FFN_CHUNK = 256
KV_ROWS = 512


def _rms(x, g):
    y = x * lax.rsqrt(jnp.mean(x * x, axis=-1, keepdims=True) + RMS_EPS)
    return y * g


def _resident(shape):
    zeros = (0,) * len(shape)
    return pl.BlockSpec(shape, lambda *_: zeros, pipeline_mode=pl.Buffered(1))


def _params(semantics, vmem_bytes):
    return pltpu.CompilerParams(dimension_semantics=semantics, vmem_limit_bytes=int(vmem_bytes))


def _mem_kv_kernel(mem_ref, g_ref, w_ref, kv_ref):
    n = _rms(mem_ref[...], g_ref[...]).astype(BF16)
    kv_ref[...] = jnp.dot(n, w_ref[...], preferred_element_type=F32).astype(BF16)


def _mem_kv(mem2d, g, w_kv):
    rows, d = mem2d.shape
    cols = w_kv.shape[1]
    return pl.pallas_call(
        _mem_kv_kernel,
        out_shape=jax.ShapeDtypeStruct((rows, cols), BF16),
        grid=(rows // KV_ROWS,),
        in_specs=[pl.BlockSpec((KV_ROWS, d), lambda i: (i, 0)), _resident((1, d)), _resident((d, cols))],
        out_specs=pl.BlockSpec((KV_ROWS, cols), lambda i: (i, 0)),
        compiler_params=_params(("parallel",), 24 * MIB),
        name="mem_kv",
    )(mem2d, g, w_kv)


def _in_proj_kernel(x_ref, pos_ref, g_ref, w_ref, freq_ref,
                    qa_ref, ka_ref, va_ref, qb_ref, kb_ref, vb_ref, ga_ref, gb_ref):
    half = ROPE_DIM // 2
    rest = HEAD_DIM // half - 2
    scale = HEAD_DIM ** -0.5
    sigmoid = lambda t: 1.0 / (1.0 + jnp.exp(-t))
    w, d = MIX_WIDTH, ga_ref.shape[1]

    for r in range(0, x_ref.shape[0], PROJ_GROUP):
        rows = slice(r, r + PROJ_GROUP)
        n = _rms(x_ref[rows, :], g_ref[...]).astype(BF16)

        ang = freq_ref[...] * pos_ref[0, :, rows]
        cos, sin = jnp.cos(ang), jnp.sin(ang)
        one, zero = jnp.ones_like(ang), jnp.zeros_like(ang)

        def table(x1, x2, other):
            return jnp.concatenate(([x1, x2] + [other] * rest) * (LANES // HEAD_DIM), axis=0).T

        cos_t = table(cos, cos, one)
        sin_lo = table(-sin, zero, zero)
        sin_hi = table(zero, sin, zero)

        def rope(t):
            tiles = []
            for c in range(t.shape[1] // LANES):
                tc = t[:, c * LANES:(c + 1) * LANES]
                tiles.append(tc * cos_t + pltpu.roll(tc, LANES - half, 1) * sin_lo
                             + pltpu.roll(tc, half, 1) * sin_hi)
            return jnp.concatenate(tiles, axis=1)

        outputs = ((qa_ref, w, lambda t: rope(t) * (scale * LOG2_E)), (ka_ref, w, rope), (va_ref, w, None),
                   (qb_ref, w, lambda t: t * scale), (kb_ref, w, None), (vb_ref, w, None),
                   (ga_ref, d, sigmoid), (gb_ref, d, sigmoid))
        col = 0
        for out_ref, width, post in outputs:
            for c in range(0, width, PROJ_CHUNK):
                t = jnp.dot(n, w_ref[:, col + c:col + c + PROJ_CHUNK], preferred_element_type=F32)
                out_ref[rows, c:c + PROJ_CHUNK] = (t if post is None else post(t)).astype(BF16)
            col += width


def _in_proj(x2d, pos_rows, g, w_in, freq):
    rows, d = x2d.shape
    tm = PROJ_ROWS
    row_block = lambda width: pl.BlockSpec((tm, width), lambda i: (i, 0))
    pos_spec = pl.BlockSpec((1, 1, tm), lambda i: (i, 0, 0))
    out_shape = [jax.ShapeDtypeStruct((rows, MIX_WIDTH), BF16)] * 6 + [jax.ShapeDtypeStruct((rows, d), BF16)] * 2
    return pl.pallas_call(
        _in_proj_kernel,
        out_shape=out_shape,
        grid=(rows // tm,),
        in_specs=[row_block(d), pos_spec, _resident((1, d)), _resident(w_in.shape), _resident(freq.shape)],
        out_specs=[row_block(MIX_WIDTH)] * 6 + [row_block(d)] * 2,
        compiler_params=_params(("parallel",), 56 * MIB),
        name="in_proj",
    )(x2d, pos_rows, g, w_in, freq)


def _split_heads(q):
    first = lax.broadcasted_iota(jnp.int32, (1, LANES), 1) < HEAD_DIM
    zero = jnp.zeros_like(q)
    return first, (jnp.where(first, q, zero), jnp.where(first, zero, q))


def _scores(qh, kj):
    return lax.dot_general(qh, kj, (((1,), (1,)), ((), ())), preferred_element_type=F32)


def _interleave(*streams):
    order = []
    for s, stream in enumerate(streams):
        total, done = sum(w for w, _ in stream), 0.0
        for k, (w, _) in enumerate(stream):
            order.append(((done + 0.5 * w) / total, s, k))
            done += w
    for _, s, k in sorted(order):
        streams[s][k][1]()


def _mixers_kernel(qa_ref, ka_ref, va_ref, qb_ref, kb_ref, vb_ref, bias_ref, tri_ref, mask_ref,
                   oa_ref, ob_ref, vones_ref, s_ref, c_ref, acc_ref, sp_ref, ls_ref, a_ref):
    blk = ATTN_BLOCK
    nq = ka_ref.shape[0] // blk
    first = lax.broadcasted_iota(jnp.int32, (1, LANES), 1) < HEAD_DIM
    block = lambda ref, i: ref[i * blk:(i + 1) * blk, :]

    tile = lambda i, j: i * (i + 1) // 2 + j
    vones_ref[:, :LANES] = va_ref[...]
    vones_ref[:, LANES:] = jnp.ones((ka_ref.shape[0], LANES), BF16)
    row_max, a_out = {}, {}

    def a_scores(i, j, h):
        s = _scores(_split_heads(block(qa_ref, i))[1][h], block(ka_ref, j)) + bias_ref[i - j]
        s_ref[h, tile(i, j)] = s
        part = jnp.maximum(s[:, :LANES], s[:, LANES:])
        row_max[i, h] = part if j == 0 else jnp.maximum(row_max[i, h], part)

    def a_output(i, h):
        m = jnp.broadcast_to(jnp.max(row_max[i, h], axis=-1, keepdims=True), (blk, LANES))
        m = jnp.concatenate([m] * (blk // LANES), axis=1)
        acc = None
        for j in range(i + 1):
            p = jnp.exp2(s_ref[h, tile(i, j)] - m).astype(BF16)
            pv = jnp.dot(p, block(vones_ref, j), preferred_element_type=F32)
            acc = pv if acc is None else acc + pv
        a_out[i, h] = acc[:, :LANES] / acc[:, LANES:]
        if h == 1:
            oa_ref[i * blk:(i + 1) * blk, :] = jnp.where(first, a_out[i, 0], a_out[i, 1]).astype(BF16)

    a_rows = []
    for i in range(nq + 1):
        row = [(1, functools.partial(a_scores, i, j, h)) for j in range(i + 1) for h in range(2)] if i < nq else []
        if i > 0:
            row += [(i, functools.partial(a_output, i - 1, h)) for h in range(2)]
        a_rows.append(row)
    a_first = sum(a_rows[:A_SPLIT], [])
    a_second = sum(a_rows[A_SPLIT:], [])

    tri = tri_ref[...]
    full_tile = ((slice(0, blk), blk),)
    diagonal_tile = ((slice(0, blk // 2), blk // 2), (slice(blk // 2, blk), blk))

    def b_sweep(blocks, key_start, d):
        parts = diagonal_tile if d is None else full_tile
        first_col, total, pending_min = {}, {}, []

        def softplus_pass(i, h, part):
            rows, keys = part
            qh = _split_heads(block(qb_ref, i))[1][h]
            z = _scores(qh[rows], kb_ref[pl.ds(key_start(i), blk), :][:keys])
            if d is None:
                z = z + mask_ref[rows, :keys]
            sp = jnp.maximum(z, 0.0) + jnp.log(1.0 + jnp.exp2(jnp.abs(z) * -LOG2_E))
            ls_ref[i, h, rows, :keys] = z - sp
            sp_ref[i, h, rows, :keys] = sp.astype(BF16)
            first_col[i, h, keys] = sp[:, :1]

        def weight_pass(i, h, part):
            rows, keys = part
            within = jnp.dot(sp_ref[i, h, rows, :keys], tri[:keys, :keys], preferred_element_type=F32)
            logit = ls_ref[i, h, rows, :keys] - within
            if d is not None:
                logit = logit - jnp.concatenate([c_ref[i, h]] * (keys // LANES), axis=1)
            a_ref[i, h, rows, :keys] = jnp.exp(logit).astype(BF16)
            total[i, h, keys] = jnp.broadcast_to(within[:, :1] + first_col[i, h, keys], (within.shape[0], LANES))

        def value_pass(i, h, part):
            rows, keys = part
            vj = vb_ref[pl.ds(key_start(i), blk), :][:keys]
            pv = jnp.dot(a_ref[i, h, rows, :keys], vj, preferred_element_type=F32)
            if d is None:
                c_ref[i, h, rows, :] = total[i, h, keys]
                acc_ref[i, h, rows, :] = pv
                return
            live = 1.0 if isinstance(d, int) else (i >= d).astype(F32)
            done = float(i <= d) if isinstance(d, int) else (i <= d).astype(F32)
            c = c_ref[i, h] + live * total[i, h, keys]
            c_ref[i, h] = c
            acc_ref[i, h] += live * pv
            pending = c + done * (2.0 * SB_DEAD_MASS)
            pending_min[:] = [pending if not pending_min else jnp.minimum(pending_min[0], pending)]

        tiles = [(i, h, part) for i in blocks for h in range(2) for part in parts]
        tasks = [(part[1], functools.partial(fn, i, h, part))
                 for fn in (softplus_pass, weight_pass, value_pass) for i, h, part in tiles]
        return tasks, pending_min

    b_diagonal, _ = b_sweep(range(nq), lambda i: i * blk, None)
    _interleave(a_first, b_diagonal)
    b_next, pending_min = b_sweep(range(1, nq), lambda i: (i - 1) * blk, 1)
    _interleave(a_second, b_next)

    def body(state):
        d, _ = state
        start = lambda i: pl.multiple_of(jnp.maximum(i - d, 0) * blk, blk)
        tasks, pending_min = b_sweep(range(2, nq), start, d)
        _interleave(tasks)
        return d + 1, (d + 1 < nq) & (jnp.min(pending_min[0]) <= SB_DEAD_MASS)

    if nq > 2:
        lax.while_loop(lambda state: state[1], body,
                       (jnp.int32(2), jnp.min(pending_min[0]) <= SB_DEAD_MASS))
    for i in range(nq):
        ob_ref[i * blk:(i + 1) * blk, :] = jnp.where(first, acc_ref[i, 0], acc_ref[i, 1]).astype(BF16)


def _mixers(qa, ka, va, qb, kb, vb, bias, tri, strict, batch, seq):
    nq = seq // ATTN_BLOCK
    spec = pl.BlockSpec((seq, LANES), lambda b, p: (b, p))
    state = pltpu.VMEM((nq, 2, ATTN_BLOCK, LANES), F32)
    tiles = lambda dtype: pltpu.VMEM((nq, 2, ATTN_BLOCK, ATTN_BLOCK), dtype)
    scratch = [pltpu.VMEM((seq, 2 * LANES), BF16),
               pltpu.VMEM((2, nq * (nq + 1) // 2, ATTN_BLOCK, ATTN_BLOCK), F32),
               state, state, tiles(BF16), tiles(F32), tiles(BF16)]
    out = jax.ShapeDtypeStruct(qa.shape, BF16)
    return pl.pallas_call(
        _mixers_kernel,
        out_shape=[out, out],
        grid=(batch, MIX_WIDTH // LANES),
        in_specs=[spec] * 6 + [_resident(bias.shape), _resident(tri.shape), _resident(strict.shape)],
        out_specs=[spec, spec],
        scratch_shapes=scratch,
        compiler_params=_params(("parallel", "parallel"), 60 * MIB),
        name="mixers",
    )(qa, ka, va, qb, kb, vb, bias, tri, strict)


def _dilated_bias(seq):
    blk = ATTN_BLOCK
    kd = np.arange(seq // blk)[:, None, None]
    dist = kd * blk + np.arange(blk)[None, :, None] - np.arange(blk)[None, None, :]
    count = np.zeros(dist.shape, np.float64)
    for window, dilation in DIL_PATTERNS:
        count += (dist >= 0) & (dist <= window) & (dist % dilation == 0)
    return np.where(count > 0, np.log2(np.maximum(count, 1.0)), MASKED).astype(np.float32)


def _sb_constants():
    blk = ATTN_BLOCK
    tri = np.arange(blk)[:, None] > np.arange(blk)[None, :]
    strict = np.arange(blk)[None, :] < np.arange(blk)[:, None]
    return jnp.asarray(tri, BF16), jnp.asarray(np.where(strict, 0.0, MASKED), F32)


def _mix_mem_kernel(x_ref, oa_ref, ob_ref, ga_ref, gb_ref, wua_ref, wub_ref, wout_ref,
                    gq_ref, wq_ref, kv_ref, wo_ref, h_ref):
    for r in range(0, x_ref.shape[0], MIX_GROUP):
        rows = slice(r, r + MIX_GROUP)
        ua = jnp.dot(oa_ref[rows, :], wua_ref[...], preferred_element_type=F32)
        ub = jnp.dot(ob_ref[rows, :], wub_ref[...], preferred_element_type=F32)
        mixed = ga_ref[rows, :].astype(F32) * ua + gb_ref[rows, :].astype(F32) * ub
        h = x_ref[rows, :] + jnp.dot(mixed.astype(BF16), wout_ref[...], preferred_element_type=F32)

        hn = _rms(h, gq_ref[...]).astype(BF16)
        q = jnp.dot(hn, wq_ref[...], preferred_element_type=F32).astype(BF16)
        heads = []
        for hd in range(N_HEADS_MEM):
            lo, hi = hd * MEM_HEAD_DIM, (hd + 1) * MEM_HEAD_DIM
            s = _scores(q[:, lo:hi], kv_ref[:, lo:hi]) * (MEM_HEAD_DIM ** -0.5)
            p = jnp.exp(s - jnp.max(s, axis=-1, keepdims=True))
            o = jnp.dot(p.astype(BF16), kv_ref[:, MEM_WIDTH + lo:MEM_WIDTH + hi], preferred_element_type=F32)
            heads.append(o / jnp.sum(p, axis=-1, keepdims=True))
        o = jnp.concatenate(heads, axis=1).astype(BF16)
        h_ref[rows, :] = h + jnp.dot(o, wo_ref[...], preferred_element_type=F32)


def _mix_mem(x2d, oa, ob, ga, gb, w_up_a, w_up_b, w_out, g_q, w_q, kv, w_o, seq, n_mem):
    rows, d = x2d.shape
    tm = MIX_ROWS
    per_seq = seq // tm
    row_block = lambda width: pl.BlockSpec((tm, width), lambda i: (i, 0))
    kv_spec = pl.BlockSpec((n_mem, kv.shape[1]), lambda i: (i // per_seq, 0))
    return pl.pallas_call(
        _mix_mem_kernel,
        out_shape=jax.ShapeDtypeStruct((rows, d), F32),
        grid=(rows // tm,),
        in_specs=[row_block(d), row_block(MIX_WIDTH), row_block(MIX_WIDTH), row_block(d), row_block(d),
                  _resident(w_up_a.shape), _resident(w_up_b.shape), _resident(w_out.shape),
                  _resident((1, d)), _resident(w_q.shape), kv_spec, _resident(w_o.shape)],
        out_specs=row_block(d),
        compiler_params=_params(("parallel",), 56 * MIB),
        name="mix_mem",
    )(x2d, oa, ob, ga, gb, w_up_a, w_up_b, w_out, g_q, w_q, kv, w_o)


def _ffn_kernel(h_ref, g_ref, wg_ref, wu_ref, wd_ref, gf_ref, o_ref, *, final_norm):
    for r in range(0, h_ref.shape[0], FFN_GROUP):
        rows = slice(r, r + FFN_GROUP)
        h = h_ref[rows, :]
        n = _rms(h, g_ref[...]).astype(BF16)
        acc = jnp.zeros(h.shape, F32)
        for c in range(0, wg_ref.shape[1], FFN_CHUNK):
            gate = jnp.dot(n, wg_ref[:, c:c + FFN_CHUNK], preferred_element_type=F32)
            up = jnp.dot(n, wu_ref[:, c:c + FFN_CHUNK], preferred_element_type=F32)
            act = (gate / (1.0 + jnp.exp(-gate)) * up).astype(BF16)
            acc = acc + jnp.dot(act, wd_ref[c:c + FFN_CHUNK, :], preferred_element_type=F32)
        h = h + acc
        o_ref[rows, :] = _rms(h, gf_ref[...]) if final_norm else h


def _ffn(h2d, g, w_gate, w_up, w_down, g_final, final_norm):
    rows, d = h2d.shape
    tm = FFN_ROWS
    row_block = pl.BlockSpec((tm, d), lambda i: (i, 0))
    return pl.pallas_call(
        functools.partial(_ffn_kernel, final_norm=final_norm),
        out_shape=jax.ShapeDtypeStruct((rows, d), F32),
        grid=(rows // tm,),
        in_specs=[row_block, _resident((1, d)), _resident(w_gate.shape), _resident(w_up.shape),
                  _resident(w_down.shape), _resident((1, d))],
        out_specs=row_block,
        compiler_params=_params(("parallel",), 48 * MIB),
        name="ffn",
    )(h2d, g, w_gate, w_up, w_down, g_final)


def kernel(x, mem, positions, g_mix, w_in, w_up_a, w_up_b, w_out, g_mem_q, g_mem_kv,
           w_q_mem, w_kv_mem, w_o_mem, g_ffn, w_ffn_gate, w_ffn_up, w_ffn_down, g_final):
    batch, seq, d = x.shape
    n_mem = mem.shape[1]
    depth = w_in.shape[0]
    d_ff = w_ffn_gate.shape[-1]
    assert seq % ATTN_BLOCK == 0 and seq % MIX_ROWS == 0 and d_ff % FFN_CHUNK == 0
    assert w_in.shape[-1] == 6 * MIX_WIDTH + 2 * d and w_kv_mem.shape[-1] == 2 * MEM_WIDTH

    row = lambda g: g.reshape(1, d)
    h = x.reshape(batch * seq, d)
    pos_rows = positions.astype(F32).reshape(batch * seq // PROJ_ROWS, 1, PROJ_ROWS)
    mem2d = mem.reshape(batch * n_mem, d)

    half = ROPE_DIM // 2
    freq = (ROPE_THETA ** (-jnp.arange(half, dtype=F32) / half)).reshape(half, 1)
    bias = jnp.asarray(_dilated_bias(seq))
    tri, strict = _sb_constants()

    for l in range(depth):
        qa, ka, va, qb, kb, vb, ga, gb = _in_proj(h, pos_rows, row(g_mix[l]), w_in[l].astype(BF16), freq)
        oa, ob = _mixers(qa, ka, va, qb, kb, vb, bias, tri, strict, batch, seq)
        kv = _mem_kv(mem2d, row(g_mem_kv[l]), w_kv_mem[l].astype(BF16))
        h = _mix_mem(h, oa, ob, ga, gb, w_up_a[l].astype(BF16), w_up_b[l].astype(BF16), w_out[l].astype(BF16),
                     row(g_mem_q[l]), w_q_mem[l].astype(BF16), kv, w_o_mem[l].astype(BF16), seq, n_mem)
        w_gate, w_up, w_down = (w[l].astype(BF16) for w in (w_ffn_gate, w_ffn_up, w_ffn_down))
        h = _ffn(h, row(g_ffn[l]), w_gate, w_up, w_down, row(g_final), final_norm=(l == depth - 1))
    return h.reshape(batch, seq, d)
```

```python
import functools

import numpy as np
import jax
import jax.numpy as jnp
from jax import lax
from jax.experimental import pallas as pl
from jax.experimental.pallas import tpu as pltpu

F32 = jnp.float32
BF16 = jnp.bfloat16

HEAD_DIM = 64
MIX_WIDTH = 512
DIL_PATTERNS = ((128, 1), (512, 4), (2048, 16))
ROPE_THETA = 500000.0
ROPE_DIM = HEAD_DIM // 4
N_HEADS_MEM = 4
MEM_HEAD_DIM = 128
MEM_WIDTH = N_HEADS_MEM * MEM_HEAD_DIM
RMS_EPS = 1e-6

LANES = 128
MIB = 1024 * 1024
V7X_VMEM_BYTES = 64 * MIB
MASKED = -1e30
LOG2_E = 1.4426950408889634
SB_DEAD_MASS = 110.0

ATTN_BLOCK = 256
A_SPLIT = 6
MIXER_LIVE_TILES = 24
PROJ_ROWS = 1024
PROJ_GROUP = 256
PROJ_CHUNK = 256
MIX_ROWS = 1024
MIX_GROUP = 512
FFN_ROWS = 1024
FFN_GROUP = 512
FFN_CHUNK = 256
KV_ROWS = 512


def _rms(x, g):
    y = x * lax.rsqrt(jnp.mean(x * x, axis=-1, keepdims=True) + RMS_EPS)
    return y * g


def _resident(shape):
    zeros = (0,) * len(shape)
    return pl.BlockSpec(shape, lambda *_: zeros, pipeline_mode=pl.Buffered(1))


def _nbytes(shape, dtype):
    return int(np.prod(shape)) * jnp.dtype(dtype).itemsize


def _params(semantics, pipelined, resident, scratch=0, live=0):
    need = 2 * pipelined + resident + scratch + live
    assert need <= V7X_VMEM_BYTES, need
    return pltpu.CompilerParams(dimension_semantics=semantics, vmem_limit_bytes=need)


def _mem_kv_kernel(mem_ref, g_ref, w_ref, kv_ref):
    n = _rms(mem_ref[...], g_ref[...]).astype(BF16)
    kv_ref[...] = jnp.dot(n, w_ref[...], preferred_element_type=F32).astype(BF16)


def _mem_kv(mem2d, g, w_kv):
    rows, d = mem2d.shape
    cols = w_kv.shape[1]
    return pl.pallas_call(
        _mem_kv_kernel,
        out_shape=jax.ShapeDtypeStruct((rows, cols), BF16),
        grid=(rows // KV_ROWS,),
        in_specs=[pl.BlockSpec((KV_ROWS, d), lambda i: (i, 0)), _resident((1, d)), _resident((d, cols))],
        out_specs=pl.BlockSpec((KV_ROWS, cols), lambda i: (i, 0)),
        compiler_params=_params(
            ("parallel",),
            pipelined=_nbytes((KV_ROWS, d), F32) + _nbytes((KV_ROWS, cols), BF16),
            resident=_nbytes((d, cols), BF16) + _nbytes((8, d), F32),
            live=_nbytes((KV_ROWS, d), F32) + _nbytes((KV_ROWS, d), BF16) + _nbytes((KV_ROWS, cols), F32)),
        name="mem_kv",
    )(mem2d, g, w_kv)


def _in_proj_kernel(x_ref, pos_ref, g_ref, w_ref, freq_ref,
                    qa_ref, ka_ref, va_ref, qb_ref, kb_ref, vb_ref, ga_ref, gb_ref):
    half = ROPE_DIM // 2
    rest = HEAD_DIM // half - 2
    scale = HEAD_DIM ** -0.5
    sigmoid = lambda t: 1.0 / (1.0 + jnp.exp(-t))
    w, d = MIX_WIDTH, ga_ref.shape[1]

    for r in range(0, x_ref.shape[0], PROJ_GROUP):
        rows = slice(r, r + PROJ_GROUP)
        n = _rms(x_ref[rows, :], g_ref[...]).astype(BF16)

        ang = freq_ref[...] * pos_ref[0, :, rows]
        cos, sin = jnp.cos(ang), jnp.sin(ang)
        one, zero = jnp.ones_like(ang), jnp.zeros_like(ang)

        def table(x1, x2, other):
            return jnp.concatenate(([x1, x2] + [other] * rest) * (LANES // HEAD_DIM), axis=0).T

        cos_t = table(cos, cos, one)
        sin_lo = table(-sin, zero, zero)
        sin_hi = table(zero, sin, zero)

        def rope(t):
            tiles = []
            for c in range(t.shape[1] // LANES):
                tc = t[:, c * LANES:(c + 1) * LANES]
                tiles.append(tc * cos_t + pltpu.roll(tc, LANES - half, 1) * sin_lo
                             + pltpu.roll(tc, half, 1) * sin_hi)
            return jnp.concatenate(tiles, axis=1)

        outputs = ((qa_ref, w, lambda t: rope(t) * (scale * LOG2_E)), (ka_ref, w, rope), (va_ref, w, None),
                   (qb_ref, w, lambda t: t * scale), (kb_ref, w, None), (vb_ref, w, None),
                   (ga_ref, d, sigmoid), (gb_ref, d, sigmoid))
        col = 0
        for out_ref, width, post in outputs:
            for c in range(0, width, PROJ_CHUNK):
                t = jnp.dot(n, w_ref[:, col + c:col + c + PROJ_CHUNK], preferred_element_type=F32)
                out_ref[rows, c:c + PROJ_CHUNK] = (t if post is None else post(t)).astype(BF16)
            col += width


def _in_proj(x2d, pos_rows, g, w_in, freq):
    rows, d = x2d.shape
    tm = PROJ_ROWS
    row_block = lambda width: pl.BlockSpec((tm, width), lambda i: (i, 0))
    pos_spec = pl.BlockSpec((1, 1, tm), lambda i: (i, 0, 0))
    out_shape = [jax.ShapeDtypeStruct((rows, MIX_WIDTH), BF16)] * 6 + [jax.ShapeDtypeStruct((rows, d), BF16)] * 2
    return pl.pallas_call(
        _in_proj_kernel,
        out_shape=out_shape,
        grid=(rows // tm,),
        in_specs=[row_block(d), pos_spec, _resident((1, d)), _resident(w_in.shape), _resident(freq.shape)],
        out_specs=[row_block(MIX_WIDTH)] * 6 + [row_block(d)] * 2,
        compiler_params=_params(
            ("parallel",),
            pipelined=(_nbytes((tm, d), F32) + _nbytes((8, tm), F32)
                       + 6 * _nbytes((tm, MIX_WIDTH), BF16) + 2 * _nbytes((tm, d), BF16)),
            resident=_nbytes(w_in.shape, BF16) + 2 * _nbytes((8, d), F32),
            live=2 * (_nbytes((PROJ_GROUP, d), F32) + _nbytes((PROJ_GROUP, d), BF16)
                      + 3 * _nbytes((PROJ_GROUP, LANES), F32) + 4 * _nbytes((PROJ_GROUP, PROJ_CHUNK), F32))),
        name="in_proj",
    )(x2d, pos_rows, g, w_in, freq)


def _scores(qh, kj):
    return lax.dot_general(qh, kj, (((1,), (1,)), ((), ())), preferred_element_type=F32)


def _interleave(*streams):
    order = []
    for s, stream in enumerate(streams):
        total, done = sum(w for w, _ in stream), 0.0
        for k, (w, _) in enumerate(stream):
            order.append(((done + 0.5 * w) / total, s, k))
            done += w
    for _, s, k in sorted(order):
        streams[s][k][1]()


def _mixers_kernel(qa_ref, ka_ref, va_ref, qb_ref, kb_ref, vb_ref, bias_ref, tri_ref, mask_ref,
                   oa_ref, ob_ref, qh_ref, vones_ref, s_ref, c_ref, acc_ref, sp_ref, ls_ref, a_ref):
    blk = ATTN_BLOCK
    nq = ka_ref.shape[0] // blk
    first = lax.broadcasted_iota(jnp.int32, (1, LANES), 1) < HEAD_DIM
    block = lambda ref, i: ref[i * blk:(i + 1) * blk, :]

    for mixer, q_ref in enumerate((qa_ref, qb_ref)):
        q = q_ref[...]
        qh_ref[mixer, 0] = jnp.where(first, q, jnp.zeros_like(q))
        qh_ref[mixer, 1] = jnp.where(first, jnp.zeros_like(q), q)

    tile = lambda i, j: i * (i + 1) // 2 + j
    vones_ref[:, :LANES] = va_ref[...]
    vones_ref[:, LANES:] = jnp.ones((ka_ref.shape[0], LANES), BF16)
    row_max, a_out = {}, {}

    def a_scores(i, j, h):
        s = _scores(qh_ref[0, h, i * blk:(i + 1) * blk, :], block(ka_ref, j)) + bias_ref[i - j]
        s_ref[h, tile(i, j)] = s
        part = jnp.maximum(s[:, :LANES], s[:, LANES:])
        row_max[i, h] = part if j == 0 else jnp.maximum(row_max[i, h], part)

    def a_output(i, h):
        m = jnp.broadcast_to(jnp.max(row_max[i, h], axis=-1, keepdims=True), (blk, LANES))
        m = jnp.concatenate([m] * (blk // LANES), axis=1)
        acc = None
        for j in range(i + 1):
            p = jnp.exp2(s_ref[h, tile(i, j)] - m).astype(BF16)
            pv = jnp.dot(p, block(vones_ref, j), preferred_element_type=F32)
            acc = pv if acc is None else acc + pv
        a_out[i, h] = acc[:, :LANES] / acc[:, LANES:]
        if h == 1:
            oa_ref[i * blk:(i + 1) * blk, :] = jnp.where(first, a_out[i, 0], a_out[i, 1]).astype(BF16)

    a_rows = []
    for i in range(nq + 1):
        row = [(1, functools.partial(a_scores, i, j, h)) for j in range(i + 1) for h in range(2)] if i < nq else []
        if i > 0:
            row += [(i, functools.partial(a_output, i - 1, h)) for h in range(2)]
        a_rows.append(row)
    a_first = sum(a_rows[:A_SPLIT], [])
    a_second = sum(a_rows[A_SPLIT:], [])

    tri = tri_ref[...]
    full_tile = ((slice(0, blk), blk),)
    diagonal_tile = ((slice(0, blk // 2), blk // 2), (slice(blk // 2, blk), blk))

    def b_sweep(blocks, key_start, d):
        parts = diagonal_tile if d is None else full_tile
        first_col, total, pending_min = {}, {}, []

        def softplus_pass(i, h, part):
            rows, keys = part
            qh = qh_ref[1, h, i * blk + rows.start:i * blk + rows.stop, :]
            z = _scores(qh, kb_ref[pl.ds(key_start(i), blk), :][:keys])
            if d is None:
                z = z + mask_ref[rows, :keys]
            sp = jnp.maximum(z, 0.0) + jnp.log(1.0 + jnp.exp2(jnp.abs(z) * -LOG2_E))
            ls_ref[i, h, rows, :keys] = z - sp
            sp_ref[i, h, rows, :keys] = sp.astype(BF16)
            first_col[i, h, keys] = sp[:, :1]

        def weight_pass(i, h, part):
            rows, keys = part
            within = jnp.dot(sp_ref[i, h, rows, :keys], tri[:keys, :keys], preferred_element_type=F32)
            logit = ls_ref[i, h, rows, :keys] - within
            if d is not None:
                logit = logit - jnp.concatenate([c_ref[i, h]] * (keys // LANES), axis=1)
            a_ref[i, h, rows, :keys] = jnp.exp(logit).astype(BF16)
            total[i, h, keys] = jnp.broadcast_to(within[:, :1] + first_col[i, h, keys], (within.shape[0], LANES))

        def value_pass(i, h, part):
            rows, keys = part
            vj = vb_ref[pl.ds(key_start(i), blk), :][:keys]
            pv = jnp.dot(a_ref[i, h, rows, :keys], vj, preferred_element_type=F32)
            if d is None:
                c_ref[i, h, rows, :] = total[i, h, keys]
                acc_ref[i, h, rows, :] = pv
                return
            live = 1.0 if isinstance(d, int) else (i >= d).astype(F32)
            done = float(i <= d) if isinstance(d, int) else (i <= d).astype(F32)
            c = c_ref[i, h] + live * total[i, h, keys]
            c_ref[i, h] = c
            acc_ref[i, h] += live * pv
            pending = c + done * (2.0 * SB_DEAD_MASS)
            pending_min[:] = [pending if not pending_min else jnp.minimum(pending_min[0], pending)]

        tiles = [(i, h, part) for i in blocks for h in range(2) for part in parts]
        tasks = [(part[1], functools.partial(fn, i, h, part))
                 for fn in (softplus_pass, weight_pass, value_pass) for i, h, part in tiles]
        return tasks, pending_min

    b_diagonal, _ = b_sweep(range(nq), lambda i: i * blk, None)
    _interleave(a_first, b_diagonal)
    b_next, pending_min = b_sweep(range(1, nq), lambda i: (i - 1) * blk, 1)
    _interleave(a_second, b_next)

    def body(state):
        d, _ = state
        start = lambda i: pl.multiple_of(jnp.maximum(i - d, 0) * blk, blk)
        tasks, pending_min = b_sweep(range(2, nq), start, d)
        _interleave(tasks)
        return d + 1, (d + 1 < nq) & (jnp.min(pending_min[0]) <= SB_DEAD_MASS)

    if nq > 2:
        lax.while_loop(lambda state: state[1], body,
                       (jnp.int32(2), jnp.min(pending_min[0]) <= SB_DEAD_MASS))
    for i in range(nq):
        ob_ref[i * blk:(i + 1) * blk, :] = jnp.where(first, acc_ref[i, 0], acc_ref[i, 1]).astype(BF16)


def _mixers(qa, ka, va, qb, kb, vb, bias, tri, strict, batch, seq):
    nq = seq // ATTN_BLOCK
    spec = pl.BlockSpec((seq, LANES), lambda b, p: (b, p))
    state = pltpu.VMEM((nq, 2, ATTN_BLOCK, LANES), F32)
    tiles = lambda dtype: pltpu.VMEM((nq, 2, ATTN_BLOCK, ATTN_BLOCK), dtype)
    scratch = [pltpu.VMEM((2, 2, seq, LANES), BF16),
               pltpu.VMEM((seq, 2 * LANES), BF16),
               pltpu.VMEM((2, nq * (nq + 1) // 2, ATTN_BLOCK, ATTN_BLOCK), F32),
               state, state, tiles(BF16), tiles(F32), tiles(BF16)]
    out = jax.ShapeDtypeStruct(qa.shape, BF16)
    return pl.pallas_call(
        _mixers_kernel,
        out_shape=[out, out],
        grid=(batch, MIX_WIDTH // LANES),
        in_specs=[spec] * 6 + [_resident(bias.shape), _resident(tri.shape), _resident(strict.shape)],
        out_specs=[spec, spec],
        scratch_shapes=scratch,
        compiler_params=_params(
            ("parallel", "parallel"),
            pipelined=8 * _nbytes((seq, LANES), BF16),
            resident=_nbytes(bias.shape, F32) + _nbytes(tri.shape, BF16) + _nbytes(strict.shape, F32),
            scratch=sum(_nbytes(buf.shape, buf.dtype) for buf in scratch),
            live=MIXER_LIVE_TILES * _nbytes((ATTN_BLOCK, ATTN_BLOCK), F32)),
        name="mixers",
    )(qa, ka, va, qb, kb, vb, bias, tri, strict)


def _dilated_bias(seq):
    blk = ATTN_BLOCK
    kd = np.arange(seq // blk)[:, None, None]
    dist = kd * blk + np.arange(blk)[None, :, None] - np.arange(blk)[None, None, :]
    count = np.zeros(dist.shape, np.float64)
    for window, dilation in DIL_PATTERNS:
        count += (dist >= 0) & (dist <= window) & (dist % dilation == 0)
    return np.where(count > 0, np.log2(np.maximum(count, 1.0)), MASKED).astype(np.float32)


def _sb_constants():
    blk = ATTN_BLOCK
    tri = np.arange(blk)[:, None] > np.arange(blk)[None, :]
    strict = np.arange(blk)[None, :] < np.arange(blk)[:, None]
    return jnp.asarray(tri, BF16), jnp.asarray(np.where(strict, 0.0, MASKED), F32)


def _mix_mem_kernel(x_ref, oa_ref, ob_ref, ga_ref, gb_ref, wua_ref, wub_ref, wout_ref,
                    gq_ref, wq_ref, kv_ref, wo_ref, h_ref):
    for r in range(0, x_ref.shape[0], MIX_GROUP):
        rows = slice(r, r + MIX_GROUP)
        ua = jnp.dot(oa_ref[rows, :], wua_ref[...], preferred_element_type=F32)
        ub = jnp.dot(ob_ref[rows, :], wub_ref[...], preferred_element_type=F32)
        mixed = ga_ref[rows, :].astype(F32) * ua + gb_ref[rows, :].astype(F32) * ub
        h = x_ref[rows, :] + jnp.dot(mixed.astype(BF16), wout_ref[...], preferred_element_type=F32)

        hn = _rms(h, gq_ref[...]).astype(BF16)
        q = jnp.dot(hn, wq_ref[...], preferred_element_type=F32).astype(BF16)
        heads = []
        for hd in range(N_HEADS_MEM):
            lo, hi = hd * MEM_HEAD_DIM, (hd + 1) * MEM_HEAD_DIM
            s = _scores(q[:, lo:hi], kv_ref[:, lo:hi]) * (MEM_HEAD_DIM ** -0.5)
            p = jnp.exp(s - jnp.max(s, axis=-1, keepdims=True))
            o = jnp.dot(p.astype(BF16), kv_ref[:, MEM_WIDTH + lo:MEM_WIDTH + hi], preferred_element_type=F32)
            heads.append(o / jnp.sum(p, axis=-1, keepdims=True))
        o = jnp.concatenate(heads, axis=1).astype(BF16)
        h_ref[rows, :] = h + jnp.dot(o, wo_ref[...], preferred_element_type=F32)


def _mix_mem(x2d, oa, ob, ga, gb, w_up_a, w_up_b, w_out, g_q, w_q, kv, w_o, seq, n_mem):
    rows, d = x2d.shape
    tm = MIX_ROWS
    per_seq = seq // tm
    row_block = lambda width: pl.BlockSpec((tm, width), lambda i: (i, 0))
    kv_spec = pl.BlockSpec((n_mem, kv.shape[1]), lambda i: (i // per_seq, 0))
    return pl.pallas_call(
        _mix_mem_kernel,
        out_shape=jax.ShapeDtypeStruct((rows, d), F32),
        grid=(rows // tm,),
        in_specs=[row_block(d), row_block(MIX_WIDTH), row_block(MIX_WIDTH), row_block(d), row_block(d),
                  _resident(w_up_a.shape), _resident(w_up_b.shape), _resident(w_out.shape),
                  _resident((1, d)), _resident(w_q.shape), kv_spec, _resident(w_o.shape)],
        out_specs=row_block(d),
        compiler_params=_params(
            ("parallel",),
            pipelined=(2 * _nbytes((tm, d), F32) + 2 * _nbytes((tm, MIX_WIDTH), BF16)
                       + 2 * _nbytes((tm, d), BF16) + _nbytes((n_mem, kv.shape[1]), BF16)),
            resident=sum(_nbytes(w.shape, BF16) for w in (w_up_a, w_up_b, w_out, w_q, w_o)) + _nbytes((8, d), F32),
            live=2 * 4 * _nbytes((MIX_GROUP, d), F32)),
        name="mix_mem",
    )(x2d, oa, ob, ga, gb, w_up_a, w_up_b, w_out, g_q, w_q, kv, w_o)


def _ffn_kernel(h_ref, g_ref, wg_ref, wu_ref, wd_ref, gf_ref, o_ref, *, final_norm):
    for r in range(0, h_ref.shape[0], FFN_GROUP):
        rows = slice(r, r + FFN_GROUP)
        h = h_ref[rows, :]
        n = _rms(h, g_ref[...]).astype(BF16)
        acc = jnp.zeros(h.shape, F32)
        for c in range(0, wg_ref.shape[1], FFN_CHUNK):
            gate = jnp.dot(n, wg_ref[:, c:c + FFN_CHUNK], preferred_element_type=F32)
            up = jnp.dot(n, wu_ref[:, c:c + FFN_CHUNK], preferred_element_type=F32)
            act = (gate / (1.0 + jnp.exp(-gate)) * up).astype(BF16)
            acc = acc + jnp.dot(act, wd_ref[c:c + FFN_CHUNK, :], preferred_element_type=F32)
        h = h + acc
        o_ref[rows, :] = _rms(h, gf_ref[...]) if final_norm else h


def _ffn(h2d, g, w_gate, w_up, w_down, g_final, final_norm):
    rows, d = h2d.shape
    tm = FFN_ROWS
    row_block = pl.BlockSpec((tm, d), lambda i: (i, 0))
    return pl.pallas_call(
        functools.partial(_ffn_kernel, final_norm=final_norm),
        out_shape=jax.ShapeDtypeStruct((rows, d), F32),
        grid=(rows // tm,),
        in_specs=[row_block, _resident((1, d)), _resident(w_gate.shape), _resident(w_up.shape),
                  _resident(w_down.shape), _resident((1, d))],
        out_specs=row_block,
        compiler_params=_params(
            ("parallel",),
            pipelined=2 * _nbytes((tm, d), F32),
            resident=sum(_nbytes(w.shape, BF16) for w in (w_gate, w_up, w_down)) + 2 * _nbytes((8, d), F32),
            live=2 * (2 * _nbytes((FFN_GROUP, d), F32) + _nbytes((FFN_GROUP, d), BF16)
                      + 3 * _nbytes((FFN_GROUP, FFN_CHUNK), F32))),
        name="ffn",
    )(h2d, g, w_gate, w_up, w_down, g_final)


def kernel(x, mem, positions, g_mix, w_in, w_up_a, w_up_b, w_out, g_mem_q, g_mem_kv,
           w_q_mem, w_kv_mem, w_o_mem, g_ffn, w_ffn_gate, w_ffn_up, w_ffn_down, g_final):
    batch, seq, d = x.shape
    n_mem = mem.shape[1]
    depth = w_in.shape[0]
    d_ff = w_ffn_gate.shape[-1]
    assert seq % ATTN_BLOCK == 0 and seq % MIX_ROWS == 0 and d_ff % FFN_CHUNK == 0
    assert w_in.shape[-1] == 6 * MIX_WIDTH + 2 * d and w_kv_mem.shape[-1] == 2 * MEM_WIDTH

    row = lambda g: g.reshape(1, d)
    h = x.reshape(batch * seq, d)
    pos_rows = positions.astype(F32).reshape(batch * seq // PROJ_ROWS, 1, PROJ_ROWS)
    mem2d = mem.reshape(batch * n_mem, d)

    half = ROPE_DIM // 2
    freq = (ROPE_THETA ** (-jnp.arange(half, dtype=F32) / half)).reshape(half, 1)
    bias = jnp.asarray(_dilated_bias(seq))
    tri, strict = _sb_constants()

    for l in range(depth):
        qa, ka, va, qb, kb, vb, ga, gb = _in_proj(h, pos_rows, row(g_mix[l]), w_in[l].astype(BF16), freq)
        oa, ob = _mixers(qa, ka, va, qb, kb, vb, bias, tri, strict, batch, seq)
        kv = _mem_kv(mem2d, row(g_mem_kv[l]), w_kv_mem[l].astype(BF16))
        h = _mix_mem(h, oa, ob, ga, gb, w_up_a[l].astype(BF16), w_up_b[l].astype(BF16), w_out[l].astype(BF16),
                     row(g_mem_q[l]), w_q_mem[l].astype(BF16), kv, w_o_mem[l].astype(BF16), seq, n_mem)
        w_gate, w_up, w_down = (w[l].astype(BF16) for w in (w_ffn_gate, w_ffn_up, w_ffn_down))
        h = _ffn(h, row(g_ffn[l]), w_gate, w_up, w_down, row(g_final), final_norm=(l == depth - 1))
    return h.reshape(batch, seq, d)
```

```python
import functools

import numpy as np
import jax
import jax.numpy as jnp
from jax import lax
from jax.experimental import pallas as pl
from jax.experimental.pallas import tpu as pltpu

F32 = jnp.float32
BF16 = jnp.bfloat16

HEAD_DIM = 64
MIX_WIDTH = 512
DIL_PATTERNS = ((128, 1), (512, 4), (2048, 16))
ROPE_THETA = 500000.0
ROPE_DIM = HEAD_DIM // 4
N_HEADS_MEM = 4
MEM_HEAD_DIM = 128
MEM_WIDTH = N_HEADS_MEM * MEM_HEAD_DIM
RMS_EPS = 1e-6

LANES = 128
BF16_SUBLANES = 16
MIB = 1024 * 1024
V7X_VMEM_BYTES = 64 * MIB
MASKED = -1e30
LOG2_E = 1.4426950408889634
SB_DEAD_MASS = 110.0

ATTN_BLOCK = 256
A_SPLIT = 6
MIXER_LIVE_TILES = 56
PROJ_ROWS = 1024
PROJ_GROUP = 256
PROJ_CHUNK = 256
MIX_ROWS = 1024
MIX_GROUP = 512
FFN_ROWS = 1024
FFN_GROUP = 512
FFN_CHUNK = 256
KV_ROWS = 512


def _rms(x, g):
    y = x * lax.rsqrt(jnp.mean(x * x, axis=-1, keepdims=True) + RMS_EPS)
    return y * g


def _resident(shape):
    zeros = (0,) * len(shape)
    return pl.BlockSpec(shape, lambda *_: zeros, pipeline_mode=pl.Buffered(1))


def _nbytes(shape, dtype):
    return int(np.prod(shape)) * jnp.dtype(dtype).itemsize


def _params(semantics, pipelined, resident, scratch=0, live=0):
    need = 2 * pipelined + resident + scratch + live
    assert need <= V7X_VMEM_BYTES, need
    return pltpu.CompilerParams(dimension_semantics=semantics, vmem_limit_bytes=need)


def _mem_kv_kernel(mem_ref, g_ref, w_ref, kv_ref):
    n = _rms(mem_ref[...], g_ref[...]).astype(BF16)
    kv_ref[...] = jnp.dot(n, w_ref[...], preferred_element_type=F32).astype(BF16)


def _mem_kv(mem2d, g, w_kv):
    rows, d = mem2d.shape
    cols = w_kv.shape[1]
    return pl.pallas_call(
        _mem_kv_kernel,
        out_shape=jax.ShapeDtypeStruct((rows, cols), BF16),
        grid=(rows // KV_ROWS,),
        in_specs=[pl.BlockSpec((KV_ROWS, d), lambda i: (i, 0)), _resident((1, d)), _resident((d, cols))],
        out_specs=pl.BlockSpec((KV_ROWS, cols), lambda i: (i, 0)),
        compiler_params=_params(
            ("parallel",),
            pipelined=_nbytes((KV_ROWS, d), F32) + _nbytes((KV_ROWS, cols), BF16),
            resident=_nbytes((d, cols), BF16) + _nbytes((8, d), F32),
            live=_nbytes((KV_ROWS, d), F32) + _nbytes((KV_ROWS, d), BF16) + _nbytes((KV_ROWS, cols), F32)),
        name="mem_kv",
    )(mem2d, g, w_kv)


def _in_proj_kernel(x_ref, pos_ref, g_ref, w_ref, freq_ref,
                    qa_ref, ka_ref, va_ref, qb_ref, kb_ref, vb_ref, ga_ref, gb_ref):
    half = ROPE_DIM // 2
    rest = HEAD_DIM // half - 2
    scale = HEAD_DIM ** -0.5
    sigmoid = lambda t: 1.0 / (1.0 + jnp.exp(-t))
    w, d = MIX_WIDTH, ga_ref.shape[1]

    for r in range(0, x_ref.shape[0], PROJ_GROUP):
        rows = slice(r, r + PROJ_GROUP)
        n = _rms(x_ref[rows, :], g_ref[...]).astype(BF16)

        ang = freq_ref[...] * pos_ref[0, :, rows]
        cos, sin = jnp.cos(ang), jnp.sin(ang)
        one, zero = jnp.ones_like(ang), jnp.zeros_like(ang)

        def table(x1, x2, other):
            return jnp.concatenate(([x1, x2] + [other] * rest) * (LANES // HEAD_DIM), axis=0).T

        cos_t = table(cos, cos, one)
        sin_lo = table(-sin, zero, zero)
        sin_hi = table(zero, sin, zero)

        def rope(t):
            tiles = []
            for c in range(t.shape[1] // LANES):
                tc = t[:, c * LANES:(c + 1) * LANES]
                tiles.append(tc * cos_t + pltpu.roll(tc, LANES - half, 1) * sin_lo
                             + pltpu.roll(tc, half, 1) * sin_hi)
            return jnp.concatenate(tiles, axis=1)

        outputs = ((qa_ref, w, lambda t: rope(t) * (scale * LOG2_E)), (ka_ref, w, rope), (va_ref, w, None),
                   (qb_ref, w, lambda t: t * scale), (kb_ref, w, None), (vb_ref, w, None),
                   (ga_ref, d, sigmoid), (gb_ref, d, sigmoid))
        col = 0
        for out_ref, width, post in outputs:
            for c in range(0, width, PROJ_CHUNK):
                t = jnp.dot(n, w_ref[:, col + c:col + c + PROJ_CHUNK], preferred_element_type=F32)
                out_ref[rows, c:c + PROJ_CHUNK] = (t if post is None else post(t)).astype(BF16)
            col += width


def _in_proj(x2d, pos_rows, g, w_in, freq):
    rows, d = x2d.shape
    tm = PROJ_ROWS
    row_block = lambda width: pl.BlockSpec((tm, width), lambda i: (i, 0))
    pos_spec = pl.BlockSpec((1, 1, tm), lambda i: (i, 0, 0))
    out_shape = [jax.ShapeDtypeStruct((rows, MIX_WIDTH), BF16)] * 6 + [jax.ShapeDtypeStruct((rows, d), BF16)] * 2
    return pl.pallas_call(
        _in_proj_kernel,
        out_shape=out_shape,
        grid=(rows // tm,),
        in_specs=[row_block(d), pos_spec, _resident((1, d)), _resident(w_in.shape), _resident(freq.shape)],
        out_specs=[row_block(MIX_WIDTH)] * 6 + [row_block(d)] * 2,
        compiler_params=_params(
            ("parallel",),
            pipelined=(_nbytes((tm, d), F32) + _nbytes((8, tm), F32)
                       + 6 * _nbytes((tm, MIX_WIDTH), BF16) + 2 * _nbytes((tm, d), BF16)),
            resident=_nbytes(w_in.shape, BF16) + 2 * _nbytes((8, d), F32),
            live=2 * (_nbytes((PROJ_GROUP, d), F32) + _nbytes((PROJ_GROUP, d), BF16)
                      + 3 * _nbytes((PROJ_GROUP, LANES), F32) + 4 * _nbytes((PROJ_GROUP, PROJ_CHUNK), F32))),
        name="in_proj",
    )(x2d, pos_rows, g, w_in, freq)


def _scores(qh, kj):
    return lax.dot_general(qh, kj, (((1,), (1,)), ((), ())), preferred_element_type=F32)


def _interleave(*streams):
    order = []
    for s, stream in enumerate(streams):
        total, done = sum(w for w, _ in stream), 0.0
        for k, (w, _) in enumerate(stream):
            order.append(((done + 0.5 * w) / total, s, k))
            done += w
    for _, s, k in sorted(order):
        streams[s][k][1]()


def _split_heads(q):
    first = lax.broadcasted_iota(jnp.int32, (1, LANES), 1) < HEAD_DIM
    zero = jnp.zeros_like(q)
    return jnp.where(first, q, zero), jnp.where(first, zero, q)


def _mixers_kernel(*refs, n_weights):
    (qa_ref, ka_ref, va_ref, qb_ref, kb_ref, vb_ref, bias_ref, tri_ref, mask_ref), refs = refs[:9], refs[9:]
    w_refs, (oa_ref, ob_ref), refs = refs[:n_weights], refs[n_weights:n_weights + 2], refs[n_weights + 2:]
    w_out_refs, (vones_ref, s_ref, c_ref, acc_ref, sp_ref, ls_ref, a_ref) = refs[:n_weights], refs[n_weights:]
    for w_ref, w_out_ref in zip(w_refs, w_out_refs):
        w_out_ref[...] = w_ref[...].astype(BF16)

    blk = ATTN_BLOCK
    nq = ka_ref.shape[0] // blk
    first = lax.broadcasted_iota(jnp.int32, (1, LANES), 1) < HEAD_DIM
    block = lambda ref, i: ref[i * blk:(i + 1) * blk, :]

    tile = lambda i, j: i * (i + 1) // 2 + j
    vones_ref[:, :LANES] = va_ref[...]
    vones_ref[:, LANES:] = jnp.ones((ka_ref.shape[0], LANES), BF16)
    row_max, a_out = {}, {}

    def a_scores(i, j, h):
        s = _scores(_split_heads(block(qa_ref, i))[h], block(ka_ref, j)) + bias_ref[i - j]
        s_ref[h, tile(i, j)] = s
        part = jnp.maximum(s[:, :LANES], s[:, LANES:])
        row_max[i, h] = part if j == 0 else jnp.maximum(row_max[i, h], part)

    def a_output(i, h):
        m = jnp.broadcast_to(jnp.max(row_max[i, h], axis=-1, keepdims=True), (blk, LANES))
        m = jnp.concatenate([m] * (blk // LANES), axis=1)
        acc = None
        for j in range(i + 1):
            p = jnp.exp2(s_ref[h, tile(i, j)] - m).astype(BF16)
            pv = jnp.dot(p, block(vones_ref, j), preferred_element_type=F32)
            acc = pv if acc is None else acc + pv
        a_out[i, h] = acc[:, :LANES] / acc[:, LANES:]
        if h == 1:
            oa_ref[i * blk:(i + 1) * blk, :] = jnp.where(first, a_out[i, 0], a_out[i, 1]).astype(BF16)

    a_rows = []
    for i in range(nq + 1):
        row = [(1, functools.partial(a_scores, i, j, h)) for j in range(i + 1) for h in range(2)] if i < nq else []
        if i > 0:
            row += [(i, functools.partial(a_output, i - 1, h)) for h in range(2)]
        a_rows.append(row)
    a_first = sum(a_rows[:A_SPLIT], [])
    a_second = sum(a_rows[A_SPLIT:], [])

    tri = tri_ref[...]
    full_tile = ((slice(0, blk), blk),)
    diagonal_tile = ((slice(0, blk // 2), blk // 2), (slice(blk // 2, blk), blk))

    def b_sweep(blocks, key_start, d):
        parts = diagonal_tile if d is None else full_tile
        first_col, total, pending_min = {}, {}, []

        def softplus_pass(i, h, part):
            rows, keys = part
            qh = _split_heads(block(qb_ref, i))[h]
            z = _scores(qh[rows], kb_ref[pl.ds(key_start(i), blk), :][:keys])
            if d is None:
                z = z + mask_ref[rows, :keys]
            sp = jnp.maximum(z, 0.0) + jnp.log(1.0 + jnp.exp2(jnp.abs(z) * -LOG2_E))
            ls_ref[i, h, rows, :keys] = z - sp
            sp_ref[i, h, rows, :keys] = sp.astype(BF16)
            first_col[i, h, keys] = sp[:, :1]

        def weight_pass(i, h, part):
            rows, keys = part
            within = jnp.dot(sp_ref[i, h, rows, :keys], tri[:keys, :keys], preferred_element_type=F32)
            logit = ls_ref[i, h, rows, :keys] - within
            if d is not None:
                logit = logit - jnp.concatenate([c_ref[i, h]] * (keys // LANES), axis=1)
            a_ref[i, h, rows, :keys] = jnp.exp(logit).astype(BF16)
            total[i, h, keys] = jnp.broadcast_to(within[:, :1] + first_col[i, h, keys], (within.shape[0], LANES))

        def value_pass(i, h, part):
            rows, keys = part
            vj = vb_ref[pl.ds(key_start(i), blk), :][:keys]
            pv = jnp.dot(a_ref[i, h, rows, :keys], vj, preferred_element_type=F32)
            if d is None:
                c_ref[i, h, rows, :] = total[i, h, keys]
                acc_ref[i, h, rows, :] = pv
                return
            live = 1.0 if isinstance(d, int) else (i >= d).astype(F32)
            done = float(i <= d) if isinstance(d, int) else (i <= d).astype(F32)
            c = c_ref[i, h] + live * total[i, h, keys]
            c_ref[i, h] = c
            acc_ref[i, h] += live * pv
            pending = c + done * (2.0 * SB_DEAD_MASS)
            pending_min[:] = [pending if not pending_min else jnp.minimum(pending_min[0], pending)]

        tiles = [(i, h, part) for i in blocks for h in range(2) for part in parts]
        tasks = [(part[1], functools.partial(fn, i, h, part))
                 for fn in (softplus_pass, weight_pass, value_pass) for i, h, part in tiles]
        return tasks, pending_min

    b_diagonal, _ = b_sweep(range(nq), lambda i: i * blk, None)
    _interleave(a_first, b_diagonal)
    b_next, pending_min = b_sweep(range(1, nq), lambda i: (i - 1) * blk, 1)
    _interleave(a_second, b_next)

    def body(state):
        d, _ = state
        start = lambda i: pl.multiple_of(jnp.maximum(i - d, 0) * blk, blk)
        tasks, pending_min = b_sweep(range(2, nq), start, d)
        _interleave(tasks)
        return d + 1, (d + 1 < nq) & (jnp.min(pending_min[0]) <= SB_DEAD_MASS)

    if nq > 2:
        lax.while_loop(lambda state: state[1], body,
                       (jnp.int32(2), jnp.min(pending_min[0]) <= SB_DEAD_MASS))
    for i in range(nq):
        ob_ref[i * blk:(i + 1) * blk, :] = jnp.where(first, acc_ref[i, 0], acc_ref[i, 1]).astype(BF16)


def _mixers(qa, ka, va, qb, kb, vb, bias, tri, strict, weights, batch, seq):
    nq = seq // ATTN_BLOCK
    pairs = MIX_WIDTH // LANES
    spec = pl.BlockSpec((seq, LANES), lambda b, p: (b, p))
    slab_rows = batch * pairs * BF16_SUBLANES
    slabs = [w.reshape(slab_rows, w.size // slab_rows) for w in weights]
    assert all(w.shape[1] % LANES == 0 for w in slabs)
    slab_specs = [pl.BlockSpec((BF16_SUBLANES, w.shape[1]), lambda b, p: (b * pairs + p, 0)) for w in slabs]
    state = pltpu.VMEM((nq, 2, ATTN_BLOCK, LANES), F32)
    tiles = lambda dtype: pltpu.VMEM((nq, 2, ATTN_BLOCK, ATTN_BLOCK), dtype)
    scratch = [pltpu.VMEM((seq, 2 * LANES), BF16),
               pltpu.VMEM((2, nq * (nq + 1) // 2, ATTN_BLOCK, ATTN_BLOCK), F32),
               state, state, tiles(BF16), tiles(F32), tiles(BF16)]
    out = jax.ShapeDtypeStruct(qa.shape, BF16)
    oa, ob, *rounded = pl.pallas_call(
        functools.partial(_mixers_kernel, n_weights=len(slabs)),
        out_shape=[out, out] + [jax.ShapeDtypeStruct(w.shape, BF16) for w in slabs],
        grid=(batch, pairs),
        in_specs=([spec] * 6 + [_resident(bias.shape), _resident(tri.shape), _resident(strict.shape)]
                  + slab_specs),
        out_specs=[spec, spec] + slab_specs,
        scratch_shapes=scratch,
        compiler_params=_params(
            ("parallel", "parallel"),
            pipelined=(8 * _nbytes((seq, LANES), BF16)
                       + sum(_nbytes((BF16_SUBLANES, w.shape[1]), F32) * 3 // 2 for w in slabs)),
            resident=_nbytes(bias.shape, F32) + _nbytes(tri.shape, BF16) + _nbytes(strict.shape, F32),
            scratch=sum(_nbytes(buf.shape, buf.dtype) for buf in scratch),
            live=MIXER_LIVE_TILES * _nbytes((ATTN_BLOCK, ATTN_BLOCK), F32)),
        name="mixers",
    )(qa, ka, va, qb, kb, vb, bias, tri, strict, *slabs)
    return oa, ob, [r.reshape(w.shape) for r, w in zip(rounded, weights)]


def _dilated_bias(seq):
    blk = ATTN_BLOCK
    kd = np.arange(seq // blk)[:, None, None]
    dist = kd * blk + np.arange(blk)[None, :, None] - np.arange(blk)[None, None, :]
    count = np.zeros(dist.shape, np.float64)
    for window, dilation in DIL_PATTERNS:
        count += (dist >= 0) & (dist <= window) & (dist % dilation == 0)
    return np.where(count > 0, np.log2(np.maximum(count, 1.0)), MASKED).astype(np.float32)


def _sb_constants():
    blk = ATTN_BLOCK
    tri = np.arange(blk)[:, None] > np.arange(blk)[None, :]
    strict = np.arange(blk)[None, :] < np.arange(blk)[:, None]
    return jnp.asarray(tri, BF16), jnp.asarray(np.where(strict, 0.0, MASKED), F32)


def _mix_mem_kernel(x_ref, oa_ref, ob_ref, ga_ref, gb_ref, wua_ref, wub_ref, wout_ref,
                    gq_ref, wq_ref, kv_ref, wo_ref, h_ref):
    for r in range(0, x_ref.shape[0], MIX_GROUP):
        rows = slice(r, r + MIX_GROUP)
        ua = jnp.dot(oa_ref[rows, :], wua_ref[...], preferred_element_type=F32)
        ub = jnp.dot(ob_ref[rows, :], wub_ref[...], preferred_element_type=F32)
        mixed = ga_ref[rows, :].astype(F32) * ua + gb_ref[rows, :].astype(F32) * ub
        h = x_ref[rows, :] + jnp.dot(mixed.astype(BF16), wout_ref[...], preferred_element_type=F32)

        hn = _rms(h, gq_ref[...]).astype(BF16)
        q = jnp.dot(hn, wq_ref[...], preferred_element_type=F32).astype(BF16)
        heads = []
        for hd in range(N_HEADS_MEM):
            lo, hi = hd * MEM_HEAD_DIM, (hd + 1) * MEM_HEAD_DIM
            s = _scores(q[:, lo:hi], kv_ref[:, lo:hi]) * (MEM_HEAD_DIM ** -0.5)
            p = jnp.exp(s - jnp.max(s, axis=-1, keepdims=True))
            o = jnp.dot(p.astype(BF16), kv_ref[:, MEM_WIDTH + lo:MEM_WIDTH + hi], preferred_element_type=F32)
            heads.append(o / jnp.sum(p, axis=-1, keepdims=True))
        o = jnp.concatenate(heads, axis=1).astype(BF16)
        h_ref[rows, :] = h + jnp.dot(o, wo_ref[...], preferred_element_type=F32)


def _mix_mem(x2d, oa, ob, ga, gb, w_up_a, w_up_b, w_out, g_q, w_q, kv, w_o, seq, n_mem):
    rows, d = x2d.shape
    tm = MIX_ROWS
    per_seq = seq // tm
    row_block = lambda width: pl.BlockSpec((tm, width), lambda i: (i, 0))
    kv_spec = pl.BlockSpec((n_mem, kv.shape[1]), lambda i: (i // per_seq, 0))
    return pl.pallas_call(
        _mix_mem_kernel,
        out_shape=jax.ShapeDtypeStruct((rows, d), F32),
        grid=(rows // tm,),
        in_specs=[row_block(d), row_block(MIX_WIDTH), row_block(MIX_WIDTH), row_block(d), row_block(d),
                  _resident(w_up_a.shape), _resident(w_up_b.shape), _resident(w_out.shape),
                  _resident((1, d)), _resident(w_q.shape), kv_spec, _resident(w_o.shape)],
        out_specs=row_block(d),
        compiler_params=_params(
            ("parallel",),
            pipelined=(2 * _nbytes((tm, d), F32) + 2 * _nbytes((tm, MIX_WIDTH), BF16)
                       + 2 * _nbytes((tm, d), BF16) + _nbytes((n_mem, kv.shape[1]), BF16)),
            resident=sum(_nbytes(w.shape, BF16) for w in (w_up_a, w_up_b, w_out, w_q, w_o)) + _nbytes((8, d), F32),
            live=2 * 4 * _nbytes((MIX_GROUP, d), F32)),
        name="mix_mem",
    )(x2d, oa, ob, ga, gb, w_up_a, w_up_b, w_out, g_q, w_q, kv, w_o)


def _ffn_kernel(h_ref, g_ref, wg_ref, wu_ref, wd_ref, gf_ref, o_ref, *, final_norm):
    for r in range(0, h_ref.shape[0], FFN_GROUP):
        rows = slice(r, r + FFN_GROUP)
        h = h_ref[rows, :]
        n = _rms(h, g_ref[...]).astype(BF16)
        acc = jnp.zeros(h.shape, F32)
        for c in range(0, wg_ref.shape[1], FFN_CHUNK):
            gate = jnp.dot(n, wg_ref[:, c:c + FFN_CHUNK], preferred_element_type=F32)
            up = jnp.dot(n, wu_ref[:, c:c + FFN_CHUNK], preferred_element_type=F32)
            act = (gate / (1.0 + jnp.exp(-gate)) * up).astype(BF16)
            acc = acc + jnp.dot(act, wd_ref[c:c + FFN_CHUNK, :], preferred_element_type=F32)
        h = h + acc
        o_ref[rows, :] = _rms(h, gf_ref[...]) if final_norm else h


def _ffn(h2d, g, w_gate, w_up, w_down, g_final, final_norm):
    rows, d = h2d.shape
    tm = FFN_ROWS
    row_block = pl.BlockSpec((tm, d), lambda i: (i, 0))
    return pl.pallas_call(
        functools.partial(_ffn_kernel, final_norm=final_norm),
        out_shape=jax.ShapeDtypeStruct((rows, d), F32),
        grid=(rows // tm,),
        in_specs=[row_block, _resident((1, d)), _resident(w_gate.shape), _resident(w_up.shape),
                  _resident(w_down.shape), _resident((1, d))],
        out_specs=row_block,
        compiler_params=_params(
            ("parallel",),
            pipelined=2 * _nbytes((tm, d), F32),
            resident=sum(_nbytes(w.shape, BF16) for w in (w_gate, w_up, w_down)) + 2 * _nbytes((8, d), F32),
            live=2 * (2 * _nbytes((FFN_GROUP, d), F32) + _nbytes((FFN_GROUP, d), BF16)
                      + 3 * _nbytes((FFN_GROUP, FFN_CHUNK), F32))),
        name="ffn",
    )(h2d, g, w_gate, w_up, w_down, g_final)


def kernel(x, mem, positions, g_mix, w_in, w_up_a, w_up_b, w_out, g_mem_q, g_mem_kv,
           w_q_mem, w_kv_mem, w_o_mem, g_ffn, w_ffn_gate, w_ffn_up, w_ffn_down, g_final):
    batch, seq, d = x.shape
    n_mem = mem.shape[1]
    depth = w_in.shape[0]
    d_ff = w_ffn_gate.shape[-1]
    assert seq % ATTN_BLOCK == 0 and seq % MIX_ROWS == 0 and d_ff % FFN_CHUNK == 0
    assert w_in.shape[-1] == 6 * MIX_WIDTH + 2 * d and w_kv_mem.shape[-1] == 2 * MEM_WIDTH

    row = lambda g: g.reshape(1, d)
    h = x.reshape(batch * seq, d)
    pos_rows = positions.astype(F32).reshape(batch * seq // PROJ_ROWS, 1, PROJ_ROWS)
    mem2d = mem.reshape(batch * n_mem, d)

    half = ROPE_DIM // 2
    freq = (ROPE_THETA ** (-jnp.arange(half, dtype=F32) / half)).reshape(half, 1)
    bias = jnp.asarray(_dilated_bias(seq))
    tri, strict = _sb_constants()

    for l in range(depth):
        qa, ka, va, qb, kb, vb, ga, gb = _in_proj(h, pos_rows, row(g_mix[l]), w_in[l].astype(BF16), freq)
        later = [w[l] for w in (w_kv_mem, w_up_a, w_up_b, w_out, w_q_mem, w_o_mem, w_ffn_gate, w_ffn_up, w_ffn_down)]
        oa, ob, (w_kv, w_ua, w_ub, w_o, w_q, w_om, w_gate, w_up, w_down) = _mixers(
            qa, ka, va, qb, kb, vb, bias, tri, strict, later, batch, seq)
        kv = _mem_kv(mem2d, row(g_mem_kv[l]), w_kv)
        h = _mix_mem(h, oa, ob, ga, gb, w_ua, w_ub, w_o, row(g_mem_q[l]), w_q, kv, w_om, seq, n_mem)
        h = _ffn(h, row(g_ffn[l]), w_gate, w_up, w_down, row(g_final), final_norm=(l == depth - 1))
    return h.reshape(batch, seq, d)
```

```python
import functools

import numpy as np
import jax
import jax.numpy as jnp
from jax import lax
from jax.experimental import pallas as pl
from jax.experimental.pallas import tpu as pltpu

F32 = jnp.float32
BF16 = jnp.bfloat16

HEAD_DIM = 64
MIX_WIDTH = 512
DIL_PATTERNS = ((128, 1), (512, 4), (2048, 16))
ROPE_THETA = 500000.0
ROPE_DIM = HEAD_DIM // 4
N_HEADS_MEM = 4
MEM_HEAD_DIM = 128
MEM_WIDTH = N_HEADS_MEM * MEM_HEAD_DIM
RMS_EPS = 1e-6

LANES = 128
MIB = 1024 * 1024
V7X_VMEM_BYTES = 64 * MIB
MASKED = -1e30
LOG2_E = 1.4426950408889634
SB_DEAD_MASS = 110.0

ATTN_BLOCK = 256
A_SPLIT = 6
MIXER_LIVE_TILES = 56
PROJ_ROWS = 1024
PROJ_GROUP = 256
PROJ_CHUNK = 256
MIX_ROWS = 1024
MIX_GROUP = 512
FFN_ROWS = 1024
FFN_GROUP = 512
FFN_CHUNK = 256
KV_ROWS = 512


def _rms(x, g):
    y = x * lax.rsqrt(jnp.mean(x * x, axis=-1, keepdims=True) + RMS_EPS)
    return y * g


def _resident(shape):
    zeros = (0,) * len(shape)
    return pl.BlockSpec(shape, lambda *_: zeros, pipeline_mode=pl.Buffered(1))


def _nbytes(shape, dtype):
    return int(np.prod(shape)) * jnp.dtype(dtype).itemsize


def _params(semantics, pipelined, resident, scratch=0, live=0):
    need = 2 * pipelined + resident + scratch + live
    assert need <= V7X_VMEM_BYTES, need
    return pltpu.CompilerParams(dimension_semantics=semantics, vmem_limit_bytes=need)


def _mem_kv_kernel(mem_ref, g_ref, w_ref, kv_ref):
    n = _rms(mem_ref[...], g_ref[...]).astype(BF16)
    kv_ref[...] = jnp.dot(n, w_ref[...], preferred_element_type=F32).astype(BF16)


def _mem_kv(mem2d, g, w_kv):
    rows, d = mem2d.shape
    cols = w_kv.shape[1]
    return pl.pallas_call(
        _mem_kv_kernel,
        out_shape=jax.ShapeDtypeStruct((rows, cols), BF16),
        grid=(rows // KV_ROWS,),
        in_specs=[pl.BlockSpec((KV_ROWS, d), lambda i: (i, 0)), _resident((1, d)), _resident((d, cols))],
        out_specs=pl.BlockSpec((KV_ROWS, cols), lambda i: (i, 0)),
        compiler_params=_params(
            ("parallel",),
            pipelined=_nbytes((KV_ROWS, d), F32) + _nbytes((KV_ROWS, cols), BF16),
            resident=_nbytes((d, cols), BF16) + _nbytes((8, d), F32),
            live=_nbytes((KV_ROWS, d), F32) + _nbytes((KV_ROWS, d), BF16) + _nbytes((KV_ROWS, cols), F32)),
        name="mem_kv",
    )(mem2d, g, w_kv)


def _in_proj_kernel(x_ref, pos_ref, g_ref, w_ref, freq_ref,
                    qa_ref, ka_ref, va_ref, qb_ref, kb_ref, vb_ref, ga_ref, gb_ref):
    half = ROPE_DIM // 2
    rest = HEAD_DIM // half - 2
    scale = HEAD_DIM ** -0.5
    sigmoid = lambda t: 1.0 / (1.0 + jnp.exp(-t))
    w, d = MIX_WIDTH, ga_ref.shape[1]

    for r in range(0, x_ref.shape[0], PROJ_GROUP):
        rows = slice(r, r + PROJ_GROUP)
        n = _rms(x_ref[rows, :], g_ref[...]).astype(BF16)

        ang = freq_ref[...] * pos_ref[0, :, rows]
        cos, sin = jnp.cos(ang), jnp.sin(ang)
        one, zero = jnp.ones_like(ang), jnp.zeros_like(ang)

        def table(x1, x2, other):
            return jnp.concatenate(([x1, x2] + [other] * rest) * (LANES // HEAD_DIM), axis=0).T

        cos_t = table(cos, cos, one)
        sin_lo = table(-sin, zero, zero)
        sin_hi = table(zero, sin, zero)

        def rope(t):
            tiles = []
            for c in range(t.shape[1] // LANES):
                tc = t[:, c * LANES:(c + 1) * LANES]
                tiles.append(tc * cos_t + pltpu.roll(tc, LANES - half, 1) * sin_lo
                             + pltpu.roll(tc, half, 1) * sin_hi)
            return jnp.concatenate(tiles, axis=1)

        outputs = ((qa_ref, w, lambda t: rope(t) * (scale * LOG2_E)), (ka_ref, w, rope), (va_ref, w, None),
                   (qb_ref, w, lambda t: t * scale), (kb_ref, w, None), (vb_ref, w, None),
                   (ga_ref, d, sigmoid), (gb_ref, d, sigmoid))
        col = 0
        for out_ref, width, post in outputs:
            for c in range(0, width, PROJ_CHUNK):
                t = jnp.dot(n, w_ref[:, col + c:col + c + PROJ_CHUNK], preferred_element_type=F32)
                out_ref[rows, c:c + PROJ_CHUNK] = (t if post is None else post(t)).astype(BF16)
            col += width


def _in_proj(x2d, pos_rows, g, w_in, freq):
    rows, d = x2d.shape
    tm = PROJ_ROWS
    row_block = lambda width: pl.BlockSpec((tm, width), lambda i: (i, 0))
    pos_spec = pl.BlockSpec((1, 1, tm), lambda i: (i, 0, 0))
    out_shape = [jax.ShapeDtypeStruct((rows, MIX_WIDTH), BF16)] * 6 + [jax.ShapeDtypeStruct((rows, d), BF16)] * 2
    return pl.pallas_call(
        _in_proj_kernel,
        out_shape=out_shape,
        grid=(rows // tm,),
        in_specs=[row_block(d), pos_spec, _resident((1, d)), _resident(w_in.shape), _resident(freq.shape)],
        out_specs=[row_block(MIX_WIDTH)] * 6 + [row_block(d)] * 2,
        compiler_params=_params(
            ("parallel",),
            pipelined=(_nbytes((tm, d), F32) + _nbytes((8, tm), F32)
                       + 6 * _nbytes((tm, MIX_WIDTH), BF16) + 2 * _nbytes((tm, d), BF16)),
            resident=_nbytes(w_in.shape, BF16) + 2 * _nbytes((8, d), F32),
            live=2 * (_nbytes((PROJ_GROUP, d), F32) + _nbytes((PROJ_GROUP, d), BF16)
                      + 3 * _nbytes((PROJ_GROUP, LANES), F32) + 4 * _nbytes((PROJ_GROUP, PROJ_CHUNK), F32))),
        name="in_proj",
    )(x2d, pos_rows, g, w_in, freq)


def _scores(qh, kj):
    return lax.dot_general(qh, kj, (((1,), (1,)), ((), ())), preferred_element_type=F32)


def _interleave(*streams):
    order = []
    for s, stream in enumerate(streams):
        total, done = sum(w for w, _ in stream), 0.0
        for k, (w, _) in enumerate(stream):
            order.append(((done + 0.5 * w) / total, s, k))
            done += w
    for _, s, k in sorted(order):
        streams[s][k][1]()


def _split_heads(q):
    first = lax.broadcasted_iota(jnp.int32, (1, LANES), 1) < HEAD_DIM
    zero = jnp.zeros_like(q)
    return jnp.where(first, q, zero), jnp.where(first, zero, q)


def _mixers_kernel(qa_ref, ka_ref, va_ref, qb_ref, kb_ref, vb_ref, bias_ref, tri_ref, mask_ref,
                   oa_ref, ob_ref, vones_ref, s_ref, c_ref, acc_ref, sp_ref, ls_ref, a_ref):
    blk = ATTN_BLOCK
    nq = ka_ref.shape[0] // blk
    first = lax.broadcasted_iota(jnp.int32, (1, LANES), 1) < HEAD_DIM
    block = lambda ref, i: ref[i * blk:(i + 1) * blk, :]

    tile = lambda i, j: i * (i + 1) // 2 + j
    vones_ref[:, :LANES] = va_ref[...]
    vones_ref[:, LANES:] = jnp.ones((ka_ref.shape[0], LANES), BF16)
    row_max, a_out = {}, {}

    def a_scores(i, j, h):
        s = _scores(_split_heads(block(qa_ref, i))[h], block(ka_ref, j)) + bias_ref[i - j]
        s_ref[h, tile(i, j)] = s
        part = jnp.maximum(s[:, :LANES], s[:, LANES:])
        row_max[i, h] = part if j == 0 else jnp.maximum(row_max[i, h], part)

    def a_output(i, h):
        m = jnp.broadcast_to(jnp.max(row_max[i, h], axis=-1, keepdims=True), (blk, LANES))
        m = jnp.concatenate([m] * (blk // LANES), axis=1)
        acc = None
        for j in range(i + 1):
            p = jnp.exp2(s_ref[h, tile(i, j)] - m).astype(BF16)
            pv = jnp.dot(p, block(vones_ref, j), preferred_element_type=F32)
            acc = pv if acc is None else acc + pv
        a_out[i, h] = acc[:, :LANES] / acc[:, LANES:]
        if h == 1:
            oa_ref[i * blk:(i + 1) * blk, :] = jnp.where(first, a_out[i, 0], a_out[i, 1]).astype(BF16)

    a_rows = []
    for i in range(nq + 1):
        row = [(1, functools.partial(a_scores, i, j, h)) for j in range(i + 1) for h in range(2)] if i < nq else []
        if i > 0:
            row += [(i, functools.partial(a_output, i - 1, h)) for h in range(2)]
        a_rows.append(row)
    a_first = sum(a_rows[:A_SPLIT], [])
    a_second = sum(a_rows[A_SPLIT:], [])

    tri = tri_ref[...]
    full_tile = ((slice(0, blk), blk),)
    diagonal_tile = ((slice(0, blk // 2), blk // 2), (slice(blk // 2, blk), blk))

    def b_sweep(blocks, key_start, d):
        parts = diagonal_tile if d is None else full_tile
        first_col, total, pending_min = {}, {}, []

        def softplus_pass(i, h, part):
            rows, keys = part
            qh = _split_heads(block(qb_ref, i))[h]
            z = _scores(qh[rows], kb_ref[pl.ds(key_start(i), blk), :][:keys])
            if d is None:
                z = z + mask_ref[rows, :keys]
            sp = jnp.maximum(z, 0.0) + jnp.log(1.0 + jnp.exp2(jnp.abs(z) * -LOG2_E))
            ls_ref[i, h, rows, :keys] = z - sp
            sp_ref[i, h, rows, :keys] = sp.astype(BF16)
            first_col[i, h, keys] = sp[:, :1]

        def weight_pass(i, h, part):
            rows, keys = part
            within = jnp.dot(sp_ref[i, h, rows, :keys], tri[:keys, :keys], preferred_element_type=F32)
            logit = ls_ref[i, h, rows, :keys] - within
            if d is not None:
                logit = logit - jnp.concatenate([c_ref[i, h]] * (keys // LANES), axis=1)
            a_ref[i, h, rows, :keys] = jnp.exp(logit).astype(BF16)
            total[i, h, keys] = jnp.broadcast_to(within[:, :1] + first_col[i, h, keys], (within.shape[0], LANES))

        def value_pass(i, h, part):
            rows, keys = part
            vj = vb_ref[pl.ds(key_start(i), blk), :][:keys]
            pv = jnp.dot(a_ref[i, h, rows, :keys], vj, preferred_element_type=F32)
            if d is None:
                c_ref[i, h, rows, :] = total[i, h, keys]
                acc_ref[i, h, rows, :] = pv
                return
            live = 1.0 if isinstance(d, int) else (i >= d).astype(F32)
            done = float(i <= d) if isinstance(d, int) else (i <= d).astype(F32)
            c = c_ref[i, h] + live * total[i, h, keys]
            c_ref[i, h] = c
            acc_ref[i, h] += live * pv
            pending = c + done * (2.0 * SB_DEAD_MASS)
            pending_min[:] = [pending if not pending_min else jnp.minimum(pending_min[0], pending)]

        tiles = [(i, h, part) for i in blocks for h in range(2) for part in parts]
        tasks = [(part[1], functools.partial(fn, i, h, part))
                 for fn in (softplus_pass, weight_pass, value_pass) for i, h, part in tiles]
        return tasks, pending_min

    b_diagonal, _ = b_sweep(range(nq), lambda i: i * blk, None)
    _interleave(a_first, b_diagonal)
    b_next, pending_min = b_sweep(range(1, nq), lambda i: (i - 1) * blk, 1)
    _interleave(a_second, b_next)

    def body(state):
        d, _ = state
        start = lambda i: pl.multiple_of(jnp.maximum(i - d, 0) * blk, blk)
        tasks, pending_min = b_sweep(range(2, nq), start, d)
        _interleave(tasks)
        return d + 1, (d + 1 < nq) & (jnp.min(pending_min[0]) <= SB_DEAD_MASS)

    if nq > 2:
        lax.while_loop(lambda state: state[1], body,
                       (jnp.int32(2), jnp.min(pending_min[0]) <= SB_DEAD_MASS))
    for i in range(nq):
        ob_ref[i * blk:(i + 1) * blk, :] = jnp.where(first, acc_ref[i, 0], acc_ref[i, 1]).astype(BF16)


def _mixers(qa, ka, va, qb, kb, vb, bias, tri, strict, batch, seq):
    nq = seq // ATTN_BLOCK
    spec = pl.BlockSpec((seq, LANES), lambda b, p: (b, p))
    state = pltpu.VMEM((nq, 2, ATTN_BLOCK, LANES), F32)
    tiles = lambda dtype: pltpu.VMEM((nq, 2, ATTN_BLOCK, ATTN_BLOCK), dtype)
    scratch = [pltpu.VMEM((seq, 2 * LANES), BF16),
               pltpu.VMEM((2, nq * (nq + 1) // 2, ATTN_BLOCK, ATTN_BLOCK), F32),
               state, state, tiles(BF16), tiles(F32), tiles(BF16)]
    out = jax.ShapeDtypeStruct(qa.shape, BF16)
    return pl.pallas_call(
        _mixers_kernel,
        out_shape=[out, out],
        grid=(batch, MIX_WIDTH // LANES),
        in_specs=[spec] * 6 + [_resident(bias.shape), _resident(tri.shape), _resident(strict.shape)],
        out_specs=[spec, spec],
        scratch_shapes=scratch,
        compiler_params=_params(
            ("parallel", "parallel"),
            pipelined=8 * _nbytes((seq, LANES), BF16),
            resident=_nbytes(bias.shape, F32) + _nbytes(tri.shape, BF16) + _nbytes(strict.shape, F32),
            scratch=sum(_nbytes(buf.shape, buf.dtype) for buf in scratch),
            live=MIXER_LIVE_TILES * _nbytes((ATTN_BLOCK, ATTN_BLOCK), F32)),
        name="mixers",
    )(qa, ka, va, qb, kb, vb, bias, tri, strict)


def _dilated_bias(seq):
    blk = ATTN_BLOCK
    kd = np.arange(seq // blk)[:, None, None]
    dist = kd * blk + np.arange(blk)[None, :, None] - np.arange(blk)[None, None, :]
    count = np.zeros(dist.shape, np.float64)
    for window, dilation in DIL_PATTERNS:
        count += (dist >= 0) & (dist <= window) & (dist % dilation == 0)
    return np.where(count > 0, np.log2(np.maximum(count, 1.0)), MASKED).astype(np.float32)


def _sb_constants():
    blk = ATTN_BLOCK
    tri = np.arange(blk)[:, None] > np.arange(blk)[None, :]
    strict = np.arange(blk)[None, :] < np.arange(blk)[:, None]
    return jnp.asarray(tri, BF16), jnp.asarray(np.where(strict, 0.0, MASKED), F32)


def _mix_mem_kernel(x_ref, oa_ref, ob_ref, ga_ref, gb_ref, wua_ref, wub_ref, wout_ref,
                    gq_ref, wq_ref, kv_ref, wo_ref, h_ref):
    for r in range(0, x_ref.shape[0], MIX_GROUP):
        rows = slice(r, r + MIX_GROUP)
        ua = jnp.dot(oa_ref[rows, :], wua_ref[...], preferred_element_type=F32)
        ub = jnp.dot(ob_ref[rows, :], wub_ref[...], preferred_element_type=F32)
        mixed = ga_ref[rows, :].astype(F32) * ua + gb_ref[rows, :].astype(F32) * ub
        h = x_ref[rows, :] + jnp.dot(mixed.astype(BF16), wout_ref[...], preferred_element_type=F32)

        hn = _rms(h, gq_ref[...]).astype(BF16)
        q = jnp.dot(hn, wq_ref[...], preferred_element_type=F32).astype(BF16)
        heads = []
        for hd in range(N_HEADS_MEM):
            lo, hi = hd * MEM_HEAD_DIM, (hd + 1) * MEM_HEAD_DIM
            s = _scores(q[:, lo:hi], kv_ref[:, lo:hi]) * (MEM_HEAD_DIM ** -0.5)
            p = jnp.exp(s - jnp.max(s, axis=-1, keepdims=True))
            o = jnp.dot(p.astype(BF16), kv_ref[:, MEM_WIDTH + lo:MEM_WIDTH + hi], preferred_element_type=F32)
            heads.append(o / jnp.sum(p, axis=-1, keepdims=True))
        o = jnp.concatenate(heads, axis=1).astype(BF16)
        h_ref[rows, :] = h + jnp.dot(o, wo_ref[...], preferred_element_type=F32)


def _mix_mem(x2d, oa, ob, ga, gb, w_up_a, w_up_b, w_out, g_q, w_q, kv, w_o, seq, n_mem):
    rows, d = x2d.shape
    tm = MIX_ROWS
    per_seq = seq // tm
    row_block = lambda width: pl.BlockSpec((tm, width), lambda i: (i, 0))
    kv_spec = pl.BlockSpec((n_mem, kv.shape[1]), lambda i: (i // per_seq, 0))
    return pl.pallas_call(
        _mix_mem_kernel,
        out_shape=jax.ShapeDtypeStruct((rows, d), F32),
        grid=(rows // tm,),
        in_specs=[row_block(d), row_block(MIX_WIDTH), row_block(MIX_WIDTH), row_block(d), row_block(d),
                  _resident(w_up_a.shape), _resident(w_up_b.shape), _resident(w_out.shape),
                  _resident((1, d)), _resident(w_q.shape), kv_spec, _resident(w_o.shape)],
        out_specs=row_block(d),
        compiler_params=_params(
            ("parallel",),
            pipelined=(2 * _nbytes((tm, d), F32) + 2 * _nbytes((tm, MIX_WIDTH), BF16)
                       + 2 * _nbytes((tm, d), BF16) + _nbytes((n_mem, kv.shape[1]), BF16)),
            resident=sum(_nbytes(w.shape, BF16) for w in (w_up_a, w_up_b, w_out, w_q, w_o)) + _nbytes((8, d), F32),
            live=2 * 4 * _nbytes((MIX_GROUP, d), F32)),
        name="mix_mem",
    )(x2d, oa, ob, ga, gb, w_up_a, w_up_b, w_out, g_q, w_q, kv, w_o)


def _ffn_kernel(h_ref, g_ref, wg_ref, wu_ref, wd_ref, gf_ref, o_ref, *, final_norm):
    for r in range(0, h_ref.shape[0], FFN_GROUP):
        rows = slice(r, r + FFN_GROUP)
        h = h_ref[rows, :]
        n = _rms(h, g_ref[...]).astype(BF16)
        acc = jnp.zeros(h.shape, F32)
        for c in range(0, wg_ref.shape[1], FFN_CHUNK):
            gate = jnp.dot(n, wg_ref[:, c:c + FFN_CHUNK], preferred_element_type=F32)
            up = jnp.dot(n, wu_ref[:, c:c + FFN_CHUNK], preferred_element_type=F32)
            act = (gate / (1.0 + jnp.exp(-gate)) * up).astype(BF16)
            acc = acc + jnp.dot(act, wd_ref[c:c + FFN_CHUNK, :], preferred_element_type=F32)
        h = h + acc
        o_ref[rows, :] = _rms(h, gf_ref[...]) if final_norm else h


def _ffn(h2d, g, w_gate, w_up, w_down, g_final, final_norm):
    rows, d = h2d.shape
    tm = FFN_ROWS
    row_block = pl.BlockSpec((tm, d), lambda i: (i, 0))
    return pl.pallas_call(
        functools.partial(_ffn_kernel, final_norm=final_norm),
        out_shape=jax.ShapeDtypeStruct((rows, d), F32),
        grid=(rows // tm,),
        in_specs=[row_block, _resident((1, d)), _resident(w_gate.shape), _resident(w_up.shape),
                  _resident(w_down.shape), _resident((1, d))],
        out_specs=row_block,
        compiler_params=_params(
            ("parallel",),
            pipelined=2 * _nbytes((tm, d), F32),
            resident=sum(_nbytes(w.shape, BF16) for w in (w_gate, w_up, w_down)) + 2 * _nbytes((8, d), F32),
            live=2 * (2 * _nbytes((FFN_GROUP, d), F32) + _nbytes((FFN_GROUP, d), BF16)
                      + 3 * _nbytes((FFN_GROUP, FFN_CHUNK), F32))),
        name="ffn",
    )(h2d, g, w_gate, w_up, w_down, g_final)


def kernel(x, mem, positions, g_mix, w_in, w_up_a, w_up_b, w_out, g_mem_q, g_mem_kv,
           w_q_mem, w_kv_mem, w_o_mem, g_ffn, w_ffn_gate, w_ffn_up, w_ffn_down, g_final):
    batch, seq, d = x.shape
    n_mem = mem.shape[1]
    depth = w_in.shape[0]
    d_ff = w_ffn_gate.shape[-1]
    assert seq % ATTN_BLOCK == 0 and seq % MIX_ROWS == 0 and d_ff % FFN_CHUNK == 0
    assert w_in.shape[-1] == 6 * MIX_WIDTH + 2 * d and w_kv_mem.shape[-1] == 2 * MEM_WIDTH

    row = lambda g: g.reshape(1, d)
    h = x.reshape(batch * seq, d)
    pos_rows = positions.astype(F32).reshape(batch * seq // PROJ_ROWS, 1, PROJ_ROWS)
    mem2d = mem.reshape(batch * n_mem, d)

    half = ROPE_DIM // 2
    freq = (ROPE_THETA ** (-jnp.arange(half, dtype=F32) / half)).reshape(half, 1)
    bias = jnp.asarray(_dilated_bias(seq))
    tri, strict = _sb_constants()

    for l in range(depth):
        qa, ka, va, qb, kb, vb, ga, gb = _in_proj(h, pos_rows, row(g_mix[l]), w_in[l].astype(BF16), freq)
        oa, ob = _mixers(qa, ka, va, qb, kb, vb, bias, tri, strict, batch, seq)
        kv = _mem_kv(mem2d, row(g_mem_kv[l]), w_kv_mem[l].astype(BF16))
        h = _mix_mem(h, oa, ob, ga, gb, w_up_a[l].astype(BF16), w_up_b[l].astype(BF16), w_out[l].astype(BF16),
                     row(g_mem_q[l]), w_q_mem[l].astype(BF16), kv, w_o_mem[l].astype(BF16), seq, n_mem)
        w_gate, w_up, w_down = (w[l].astype(BF16) for w in (w_ffn_gate, w_ffn_up, w_ffn_down))
        h = _ffn(h, row(g_ffn[l]), w_gate, w_up, w_down, row(g_final), final_norm=(l == depth - 1))
    return h.reshape(batch, seq, d)
```

```python
import functools

import numpy as np
import jax
import jax.numpy as jnp
from jax import lax
from jax.experimental import pallas as pl
from jax.experimental.pallas import tpu as pltpu

F32 = jnp.float32
BF16 = jnp.bfloat16

HEAD_DIM = 64
MIX_WIDTH = 512
DIL_PATTERNS = ((128, 1), (512, 4), (2048, 16))
ROPE_THETA = 500000.0
ROPE_DIM = HEAD_DIM // 4
N_HEADS_MEM = 4
MEM_HEAD_DIM = 128
MEM_WIDTH = N_HEADS_MEM * MEM_HEAD_DIM
RMS_EPS = 1e-6

LANES = 128
MIB = 1024 * 1024
V7X_VMEM_BYTES = 64 * MIB
MASKED = -1e30
LOG2_E = 1.4426950408889634
SB_DEAD_MASS = 110.0

ATTN_BLOCK = 256
A_SPLIT = 6
B_TAIL_ROWS = 64
MIXER_LIVE_TILES = 56
PROJ_ROWS = 1024
PROJ_GROUP = 256
PROJ_CHUNK = 256
MIX_ROWS = 1024
MIX_GROUP = 512
FFN_ROWS = 1024
FFN_GROUP = 512
FFN_CHUNK = 256
KV_ROWS = 512


def _rms(x, g):
    y = x * lax.rsqrt(jnp.mean(x * x, axis=-1, keepdims=True) + RMS_EPS)
    return y * g


def _resident(shape):
    zeros = (0,) * len(shape)
    return pl.BlockSpec(shape, lambda *_: zeros, pipeline_mode=pl.Buffered(1))


def _nbytes(shape, dtype):
    return int(np.prod(shape)) * jnp.dtype(dtype).itemsize


def _params(semantics, pipelined, resident, scratch=0, live=0):
    need = 2 * pipelined + resident + scratch + live
    assert need <= V7X_VMEM_BYTES, need
    return pltpu.CompilerParams(dimension_semantics=semantics, vmem_limit_bytes=need)


def _mem_kv_kernel(mem_ref, g_ref, w_ref, kv_ref):
    n = _rms(mem_ref[...], g_ref[...]).astype(BF16)
    kv_ref[...] = jnp.dot(n, w_ref[...], preferred_element_type=F32).astype(BF16)


def _mem_kv(mem2d, g, w_kv):
    rows, d = mem2d.shape
    cols = w_kv.shape[1]
    return pl.pallas_call(
        _mem_kv_kernel,
        out_shape=jax.ShapeDtypeStruct((rows, cols), BF16),
        grid=(rows // KV_ROWS,),
        in_specs=[pl.BlockSpec((KV_ROWS, d), lambda i: (i, 0)), _resident((1, d)), _resident((d, cols))],
        out_specs=pl.BlockSpec((KV_ROWS, cols), lambda i: (i, 0)),
        compiler_params=_params(
            ("parallel",),
            pipelined=_nbytes((KV_ROWS, d), F32) + _nbytes((KV_ROWS, cols), BF16),
            resident=_nbytes((d, cols), BF16) + _nbytes((8, d), F32),
            live=_nbytes((KV_ROWS, d), F32) + _nbytes((KV_ROWS, d), BF16) + _nbytes((KV_ROWS, cols), F32)),
        name="mem_kv",
    )(mem2d, g, w_kv)


def _in_proj_kernel(x_ref, pos_ref, g_ref, w_ref, freq_ref,
                    qa_ref, ka_ref, va_ref, qb_ref, kb_ref, vb_ref, ga_ref, gb_ref):
    half = ROPE_DIM // 2
    rest = HEAD_DIM // half - 2
    scale = HEAD_DIM ** -0.5
    sigmoid = lambda t: 1.0 / (1.0 + jnp.exp(-t))
    w, d = MIX_WIDTH, ga_ref.shape[1]

    for r in range(0, x_ref.shape[0], PROJ_GROUP):
        rows = slice(r, r + PROJ_GROUP)
        n = _rms(x_ref[rows, :], g_ref[...]).astype(BF16)

        ang = freq_ref[...] * pos_ref[0, :, rows]
        cos, sin = jnp.cos(ang), jnp.sin(ang)
        one, zero = jnp.ones_like(ang), jnp.zeros_like(ang)

        def table(x1, x2, other):
            return jnp.concatenate(([x1, x2] + [other] * rest) * (LANES // HEAD_DIM), axis=0).T

        cos_t = table(cos, cos, one)
        sin_lo = table(-sin, zero, zero)
        sin_hi = table(zero, sin, zero)

        def rope(t):
            tiles = []
            for c in range(t.shape[1] // LANES):
                tc = t[:, c * LANES:(c + 1) * LANES]
                tiles.append(tc * cos_t + pltpu.roll(tc, LANES - half, 1) * sin_lo
                             + pltpu.roll(tc, half, 1) * sin_hi)
            return jnp.concatenate(tiles, axis=1)

        outputs = ((qa_ref, w, lambda t: rope(t) * (scale * LOG2_E)), (ka_ref, w, rope), (va_ref, w, None),
                   (qb_ref, w, lambda t: t * scale), (kb_ref, w, None), (vb_ref, w, None),
                   (ga_ref, d, sigmoid), (gb_ref, d, sigmoid))
        col = 0
        for out_ref, width, post in outputs:
            for c in range(0, width, PROJ_CHUNK):
                t = jnp.dot(n, w_ref[:, col + c:col + c + PROJ_CHUNK], preferred_element_type=F32)
                out_ref[rows, c:c + PROJ_CHUNK] = (t if post is None else post(t)).astype(BF16)
            col += width


def _in_proj(x2d, pos_rows, g, w_in, freq):
    rows, d = x2d.shape
    tm = PROJ_ROWS
    row_block = lambda width: pl.BlockSpec((tm, width), lambda i: (i, 0))
    pos_spec = pl.BlockSpec((1, 1, tm), lambda i: (i, 0, 0))
    out_shape = [jax.ShapeDtypeStruct((rows, MIX_WIDTH), BF16)] * 6 + [jax.ShapeDtypeStruct((rows, d), BF16)] * 2
    return pl.pallas_call(
        _in_proj_kernel,
        out_shape=out_shape,
        grid=(rows // tm,),
        in_specs=[row_block(d), pos_spec, _resident((1, d)), _resident(w_in.shape), _resident(freq.shape)],
        out_specs=[row_block(MIX_WIDTH)] * 6 + [row_block(d)] * 2,
        compiler_params=_params(
            ("parallel",),
            pipelined=(_nbytes((tm, d), F32) + _nbytes((8, tm), F32)
                       + 6 * _nbytes((tm, MIX_WIDTH), BF16) + 2 * _nbytes((tm, d), BF16)),
            resident=_nbytes(w_in.shape, BF16) + 2 * _nbytes((8, d), F32),
            live=2 * (_nbytes((PROJ_GROUP, d), F32) + _nbytes((PROJ_GROUP, d), BF16)
                      + 3 * _nbytes((PROJ_GROUP, LANES), F32) + 4 * _nbytes((PROJ_GROUP, PROJ_CHUNK), F32))),
        name="in_proj",
    )(x2d, pos_rows, g, w_in, freq)


def _scores(qh, kj):
    return lax.dot_general(qh, kj, (((1,), (1,)), ((), ())), preferred_element_type=F32)


def _interleave(*streams):
    order = []
    for s, stream in enumerate(streams):
        total, done = sum(w for w, _ in stream), 0.0
        for k, (w, _) in enumerate(stream):
            order.append(((done + 0.5 * w) / total, s, k))
            done += w
    for _, s, k in sorted(order):
        streams[s][k][1]()


def _split_heads(q):
    first = lax.broadcasted_iota(jnp.int32, (1, LANES), 1) < HEAD_DIM
    zero = jnp.zeros_like(q)
    return jnp.where(first, q, zero), jnp.where(first, zero, q)


def _mixers_kernel(qa_ref, ka_ref, va_ref, qb_ref, kb_ref, vb_ref, bias_ref, tri_ref, mask_ref,
                   oa_ref, ob_ref, vones_ref, s_ref, c_ref, acc_ref, sp_ref, ls_ref, a_ref):
    blk = ATTN_BLOCK
    nq = ka_ref.shape[0] // blk
    first = lax.broadcasted_iota(jnp.int32, (1, LANES), 1) < HEAD_DIM
    block = lambda ref, i: ref[i * blk:(i + 1) * blk, :]

    tile = lambda i, j: i * (i + 1) // 2 + j
    vones_ref[:, :LANES] = va_ref[...]
    vones_ref[:, LANES:] = jnp.ones((ka_ref.shape[0], LANES), BF16)
    row_max, a_out = {}, {}

    def a_scores(i, j, h):
        s = _scores(_split_heads(block(qa_ref, i))[h], block(ka_ref, j)) + bias_ref[i - j]
        s_ref[h, tile(i, j)] = s
        part = jnp.maximum(s[:, :LANES], s[:, LANES:])
        row_max[i, h] = part if j == 0 else jnp.maximum(row_max[i, h], part)

    def a_output(i, h):
        m = jnp.broadcast_to(jnp.max(row_max[i, h], axis=-1, keepdims=True), (blk, LANES))
        m = jnp.concatenate([m] * (blk // LANES), axis=1)
        acc = None
        for j in range(i + 1):
            p = jnp.exp2(s_ref[h, tile(i, j)] - m).astype(BF16)
            pv = jnp.dot(p, block(vones_ref, j), preferred_element_type=F32)
            acc = pv if acc is None else acc + pv
        a_out[i, h] = acc[:, :LANES] / acc[:, LANES:]
        if h == 1:
            oa_ref[i * blk:(i + 1) * blk, :] = jnp.where(first, a_out[i, 0], a_out[i, 1]).astype(BF16)

    a_rows = []
    for i in range(nq + 1):
        row = [(1, functools.partial(a_scores, i, j, h)) for j in range(i + 1) for h in range(2)] if i < nq else []
        if i > 0:
            row += [(i, functools.partial(a_output, i - 1, h)) for h in range(2)]
        a_rows.append(row)
    a_first = sum(a_rows[:A_SPLIT], [])
    a_second = sum(a_rows[A_SPLIT:], [])

    tri = tri_ref[...]
    full_tile = ((slice(0, blk), blk),)
    diagonal_tile = ((slice(0, blk // 2), blk // 2), (slice(blk // 2, blk), blk))
    head_rows, tail_rows = ((slice(0, blk - B_TAIL_ROWS), blk),), ((slice(blk - B_TAIL_ROWS, blk), blk),)

    def b_sweep(blocks, key_start, d, parts):
        first_col, total = {}, {}

        def softplus_pass(i, h, part):
            rows, keys = part
            qh = _split_heads(block(qb_ref, i))[h]
            z = _scores(qh[rows], kb_ref[pl.ds(key_start(i), blk), :][:keys])
            if d is None:
                z = z + mask_ref[rows, :keys]
            sp = jnp.maximum(z, 0.0) + jnp.log(1.0 + jnp.exp2(jnp.abs(z) * -LOG2_E))
            ls_ref[i, h, rows, :keys] = z - sp
            sp_ref[i, h, rows, :keys] = sp.astype(BF16)
            first_col[i, h, rows.start] = sp[:, :1]

        def weight_pass(i, h, part):
            rows, keys = part
            within = jnp.dot(sp_ref[i, h, rows, :keys], tri[:keys, :keys], preferred_element_type=F32)
            logit = ls_ref[i, h, rows, :keys] - within
            if d is not None:
                logit = logit - jnp.concatenate([c_ref[i, h, rows, :]] * (keys // LANES), axis=1)
            a_ref[i, h, rows, :keys] = jnp.exp(logit).astype(BF16)
            total[i, h, rows.start] = jnp.broadcast_to(within[:, :1] + first_col[i, h, rows.start],
                                                       (within.shape[0], LANES))

        def value_pass(i, h, part):
            rows, keys = part
            vj = vb_ref[pl.ds(key_start(i), blk), :][:keys]
            pv = jnp.dot(a_ref[i, h, rows, :keys], vj, preferred_element_type=F32)
            if d is None:
                c_ref[i, h, rows, :] = total[i, h, rows.start]
                acc_ref[i, h, rows, :] = pv
                return
            live = 1.0 if isinstance(d, int) else (i >= d).astype(F32)
            c_ref[i, h, rows, :] += live * total[i, h, rows.start]
            acc_ref[i, h, rows, :] += live * pv

        tiles = [(i, h, part) for i in blocks for h in range(2) for part in parts]
        return [(part[1], functools.partial(fn, i, h, part))
                for fn in (softplus_pass, weight_pass, value_pass) for i, h, part in tiles]

    def min_mass(blocks, rows, unfinished=None):
        least = None
        for i in blocks:
            for h in range(2):
                c = c_ref[i, h, rows, :]
                if unfinished is not None:
                    c = c + (1.0 - unfinished(i)) * (2.0 * SB_DEAD_MASS)
                least = c if least is None else jnp.minimum(least, c)
        return jnp.min(least)

    _interleave(a_first, b_sweep(range(nq), lambda i: i * blk, None, diagonal_tile))
    previous = lambda i: (i - 1) * blk
    _interleave(a_second, b_sweep(range(1, nq), previous, 1, head_rows))

    @pl.when(min_mass(range(1, nq), tail_rows[0][0]) <= SB_DEAD_MASS)
    def _():
        _interleave(b_sweep(range(1, nq), previous, 1, tail_rows))

    def body(state):
        d, _ = state
        start = lambda i: pl.multiple_of(jnp.maximum(i - d, 0) * blk, blk)
        _interleave(b_sweep(range(2, nq), start, d, full_tile))
        alive = min_mass(range(2, nq), slice(0, blk), lambda i: (i > d).astype(F32)) <= SB_DEAD_MASS
        return d + 1, (d + 1 < nq) & alive

    if nq > 2:
        lax.while_loop(lambda state: state[1], body,
                       (jnp.int32(2), min_mass(range(2, nq), slice(0, blk)) <= SB_DEAD_MASS))
    for i in range(nq):
        ob_ref[i * blk:(i + 1) * blk, :] = jnp.where(first, acc_ref[i, 0], acc_ref[i, 1]).astype(BF16)


def _mixers(qa, ka, va, qb, kb, vb, bias, tri, strict, batch, seq):
    nq = seq // ATTN_BLOCK
    spec = pl.BlockSpec((seq, LANES), lambda b, p: (b, p))
    state = pltpu.VMEM((nq, 2, ATTN_BLOCK, LANES), F32)
    tiles = lambda dtype: pltpu.VMEM((nq, 2, ATTN_BLOCK, ATTN_BLOCK), dtype)
    scratch = [pltpu.VMEM((seq, 2 * LANES), BF16),
               pltpu.VMEM((2, nq * (nq + 1) // 2, ATTN_BLOCK, ATTN_BLOCK), F32),
               state, state, tiles(BF16), tiles(F32), tiles(BF16)]
    out = jax.ShapeDtypeStruct(qa.shape, BF16)
    return pl.pallas_call(
        _mixers_kernel,
        out_shape=[out, out],
        grid=(batch, MIX_WIDTH // LANES),
        in_specs=[spec] * 6 + [_resident(bias.shape), _resident(tri.shape), _resident(strict.shape)],
        out_specs=[spec, spec],
        scratch_shapes=scratch,
        compiler_params=_params(
            ("parallel", "parallel"),
            pipelined=8 * _nbytes((seq, LANES), BF16),
            resident=_nbytes(bias.shape, F32) + _nbytes(tri.shape, BF16) + _nbytes(strict.shape, F32),
            scratch=sum(_nbytes(buf.shape, buf.dtype) for buf in scratch),
            live=MIXER_LIVE_TILES * _nbytes((ATTN_BLOCK, ATTN_BLOCK), F32)),
        name="mixers",
    )(qa, ka, va, qb, kb, vb, bias, tri, strict)


def _dilated_bias(seq):
    blk = ATTN_BLOCK
    kd = np.arange(seq // blk)[:, None, None]
    dist = kd * blk + np.arange(blk)[None, :, None] - np.arange(blk)[None, None, :]
    count = np.zeros(dist.shape, np.float64)
    for window, dilation in DIL_PATTERNS:
        count += (dist >= 0) & (dist <= window) & (dist % dilation == 0)
    return np.where(count > 0, np.log2(np.maximum(count, 1.0)), MASKED).astype(np.float32)


def _sb_constants():
    blk = ATTN_BLOCK
    tri = np.arange(blk)[:, None] > np.arange(blk)[None, :]
    strict = np.arange(blk)[None, :] < np.arange(blk)[:, None]
    return jnp.asarray(tri, BF16), jnp.asarray(np.where(strict, 0.0, MASKED), F32)


def _mix_mem_kernel(x_ref, oa_ref, ob_ref, ga_ref, gb_ref, wua_ref, wub_ref, wout_ref,
                    gq_ref, wq_ref, kv_ref, wo_ref, h_ref):
    for r in range(0, x_ref.shape[0], MIX_GROUP):
        rows = slice(r, r + MIX_GROUP)
        ua = jnp.dot(oa_ref[rows, :], wua_ref[...], preferred_element_type=F32)
        ub = jnp.dot(ob_ref[rows, :], wub_ref[...], preferred_element_type=F32)
        mixed = ga_ref[rows, :].astype(F32) * ua + gb_ref[rows, :].astype(F32) * ub
        h = x_ref[rows, :] + jnp.dot(mixed.astype(BF16), wout_ref[...], preferred_element_type=F32)

        hn = _rms(h, gq_ref[...]).astype(BF16)
        q = jnp.dot(hn, wq_ref[...], preferred_element_type=F32).astype(BF16)
        heads = []
        for hd in range(N_HEADS_MEM):
            lo, hi = hd * MEM_HEAD_DIM, (hd + 1) * MEM_HEAD_DIM
            s = _scores(q[:, lo:hi], kv_ref[:, lo:hi]) * (MEM_HEAD_DIM ** -0.5)
            p = jnp.exp(s - jnp.max(s, axis=-1, keepdims=True))
            o = jnp.dot(p.astype(BF16), kv_ref[:, MEM_WIDTH + lo:MEM_WIDTH + hi], preferred_element_type=F32)
            heads.append(o / jnp.sum(p, axis=-1, keepdims=True))
        o = jnp.concatenate(heads, axis=1).astype(BF16)
        h_ref[rows, :] = h + jnp.dot(o, wo_ref[...], preferred_element_type=F32)


def _mix_mem(x2d, oa, ob, ga, gb, w_up_a, w_up_b, w_out, g_q, w_q, kv, w_o, seq, n_mem):
    rows, d = x2d.shape
    tm = MIX_ROWS
    per_seq = seq // tm
    row_block = lambda width: pl.BlockSpec((tm, width), lambda i: (i, 0))
    kv_spec = pl.BlockSpec((n_mem, kv.shape[1]), lambda i: (i // per_seq, 0))
    return pl.pallas_call(
        _mix_mem_kernel,
        out_shape=jax.ShapeDtypeStruct((rows, d), F32),
        grid=(rows // tm,),
        in_specs=[row_block(d), row_block(MIX_WIDTH), row_block(MIX_WIDTH), row_block(d), row_block(d),
                  _resident(w_up_a.shape), _resident(w_up_b.shape), _resident(w_out.shape),
                  _resident((1, d)), _resident(w_q.shape), kv_spec, _resident(w_o.shape)],
        out_specs=row_block(d),
        compiler_params=_params(
            ("parallel",),
            pipelined=(2 * _nbytes((tm, d), F32) + 2 * _nbytes((tm, MIX_WIDTH), BF16)
                       + 2 * _nbytes((tm, d), BF16) + _nbytes((n_mem, kv.shape[1]), BF16)),
            resident=sum(_nbytes(w.shape, BF16) for w in (w_up_a, w_up_b, w_out, w_q, w_o)) + _nbytes((8, d), F32),
            live=2 * 4 * _nbytes((MIX_GROUP, d), F32)),
        name="mix_mem",
    )(x2d, oa, ob, ga, gb, w_up_a, w_up_b, w_out, g_q, w_q, kv, w_o)


def _ffn_kernel(h_ref, g_ref, wg_ref, wu_ref, wd_ref, gf_ref, o_ref, *, final_norm):
    for r in range(0, h_ref.shape[0], FFN_GROUP):
        rows = slice(r, r + FFN_GROUP)
        h = h_ref[rows, :]
        n = _rms(h, g_ref[...]).astype(BF16)
        acc = jnp.zeros(h.shape, F32)
        for c in range(0, wg_ref.shape[1], FFN_CHUNK):
            gate = jnp.dot(n, wg_ref[:, c:c + FFN_CHUNK], preferred_element_type=F32)
            up = jnp.dot(n, wu_ref[:, c:c + FFN_CHUNK], preferred_element_type=F32)
            act = (gate / (1.0 + jnp.exp(-gate)) * up).astype(BF16)
            acc = acc + jnp.dot(act, wd_ref[c:c + FFN_CHUNK, :], preferred_element_type=F32)
        h = h + acc
        o_ref[rows, :] = _rms(h, gf_ref[...]) if final_norm else h


def _ffn(h2d, g, w_gate, w_up, w_down, g_final, final_norm):
    rows, d = h2d.shape
    tm = FFN_ROWS
    row_block = pl.BlockSpec((tm, d), lambda i: (i, 0))
    return pl.pallas_call(
        functools.partial(_ffn_kernel, final_norm=final_norm),
        out_shape=jax.ShapeDtypeStruct((rows, d), F32),
        grid=(rows // tm,),
        in_specs=[row_block, _resident((1, d)), _resident(w_gate.shape), _resident(w_up.shape),
                  _resident(w_down.shape), _resident((1, d))],
        out_specs=row_block,
        compiler_params=_params(
            ("parallel",),
            pipelined=2 * _nbytes((tm, d), F32),
            resident=sum(_nbytes(w.shape, BF16) for w in (w_gate, w_up, w_down)) + 2 * _nbytes((8, d), F32),
            live=2 * (2 * _nbytes((FFN_GROUP, d), F32) + _nbytes((FFN_GROUP, d), BF16)
                      + 3 * _nbytes((FFN_GROUP, FFN_CHUNK), F32))),
        name="ffn",
    )(h2d, g, w_gate, w_up, w_down, g_final)


def kernel(x, mem, positions, g_mix, w_in, w_up_a, w_up_b, w_out, g_mem_q, g_mem_kv,
           w_q_mem, w_kv_mem, w_o_mem, g_ffn, w_ffn_gate, w_ffn_up, w_ffn_down, g_final):
    batch, seq, d = x.shape
    n_mem = mem.shape[1]
    depth = w_in.shape[0]
    d_ff = w_ffn_gate.shape[-1]
    assert seq % ATTN_BLOCK == 0 and seq % MIX_ROWS == 0 and d_ff % FFN_CHUNK == 0
    assert w_in.shape[-1] == 6 * MIX_WIDTH + 2 * d and w_kv_mem.shape[-1] == 2 * MEM_WIDTH

    row = lambda g: g.reshape(1, d)
    h = x.reshape(batch * seq, d)
    pos_rows = positions.astype(F32).reshape(batch * seq // PROJ_ROWS, 1, PROJ_ROWS)
    mem2d = mem.reshape(batch * n_mem, d)

    half = ROPE_DIM // 2
    freq = (ROPE_THETA ** (-jnp.arange(half, dtype=F32) / half)).reshape(half, 1)
    bias = jnp.asarray(_dilated_bias(seq))
    tri, strict = _sb_constants()

    for l in range(depth):
        qa, ka, va, qb, kb, vb, ga, gb = _in_proj(h, pos_rows, row(g_mix[l]), w_in[l].astype(BF16), freq)
        oa, ob = _mixers(qa, ka, va, qb, kb, vb, bias, tri, strict, batch, seq)
        kv = _mem_kv(mem2d, row(g_mem_kv[l]), w_kv_mem[l].astype(BF16))
        h = _mix_mem(h, oa, ob, ga, gb, w_up_a[l].astype(BF16), w_up_b[l].astype(BF16), w_out[l].astype(BF16),
                     row(g_mem_q[l]), w_q_mem[l].astype(BF16), kv, w_o_mem[l].astype(BF16), seq, n_mem)
        w_gate, w_up, w_down = (w[l].astype(BF16) for w in (w_ffn_gate, w_ffn_up, w_ffn_down))
        h = _ffn(h, row(g_ffn[l]), w_gate, w_up, w_down, row(g_final), final_norm=(l == depth - 1))
    return h.reshape(batch, seq, d)
```

```python
import functools

import numpy as np
import jax
import jax.numpy as jnp
from jax import lax
from jax.experimental import pallas as pl
from jax.experimental.pallas import tpu as pltpu

F32 = jnp.float32
BF16 = jnp.bfloat16

HEAD_DIM = 64
MIX_WIDTH = 512
DIL_PATTERNS = ((128, 1), (512, 4), (2048, 16))
ROPE_THETA = 500000.0
ROPE_DIM = HEAD_DIM // 4
N_HEADS_MEM = 4
MEM_HEAD_DIM = 128
MEM_WIDTH = N_HEADS_MEM * MEM_HEAD_DIM
RMS_EPS = 1e-6

LANES = 128
MIB = 1024 * 1024
V7X_VMEM_BYTES = 64 * MIB
MASKED = -1e30
LOG2_E = 1.4426950408889634
SB_DEAD_MASS = 110.0

ATTN_BLOCK = 256
A_SPLIT = 6
B_NEAR_ROWS = 176
B_FAR_ROWS = 48
MIXER_LIVE_TILES = 56
PROJ_ROWS = 1024
PROJ_GROUP = 256
PROJ_CHUNK = 256
MIX_ROWS = 1024
MIX_GROUP = 512
FFN_ROWS = 1024
FFN_GROUP = 512
FFN_CHUNK = 256
KV_ROWS = 512


def _rms(x, g):
    y = x * lax.rsqrt(jnp.mean(x * x, axis=-1, keepdims=True) + RMS_EPS)
    return y * g


def _resident(shape):
    zeros = (0,) * len(shape)
    return pl.BlockSpec(shape, lambda *_: zeros, pipeline_mode=pl.Buffered(1))


def _nbytes(shape, dtype):
    return int(np.prod(shape)) * jnp.dtype(dtype).itemsize


def _params(semantics, pipelined, resident, scratch=0, live=0):
    need = 2 * pipelined + resident + scratch + live
    assert need <= V7X_VMEM_BYTES, need
    return pltpu.CompilerParams(dimension_semantics=semantics, vmem_limit_bytes=need)


def _mem_kv_kernel(mem_ref, g_ref, w_ref, kv_ref):
    n = _rms(mem_ref[...], g_ref[...]).astype(BF16)
    kv_ref[...] = jnp.dot(n, w_ref[...], preferred_element_type=F32).astype(BF16)


def _mem_kv(mem2d, g, w_kv):
    rows, d = mem2d.shape
    cols = w_kv.shape[1]
    return pl.pallas_call(
        _mem_kv_kernel,
        out_shape=jax.ShapeDtypeStruct((rows, cols), BF16),
        grid=(rows // KV_ROWS,),
        in_specs=[pl.BlockSpec((KV_ROWS, d), lambda i: (i, 0)), _resident((1, d)), _resident((d, cols))],
        out_specs=pl.BlockSpec((KV_ROWS, cols), lambda i: (i, 0)),
        compiler_params=_params(
            ("parallel",),
            pipelined=_nbytes((KV_ROWS, d), F32) + _nbytes((KV_ROWS, cols), BF16),
            resident=_nbytes((d, cols), BF16) + _nbytes((8, d), F32),
            live=_nbytes((KV_ROWS, d), F32) + _nbytes((KV_ROWS, d), BF16) + _nbytes((KV_ROWS, cols), F32)),
        name="mem_kv",
    )(mem2d, g, w_kv)


def _in_proj_kernel(x_ref, pos_ref, g_ref, w_ref, freq_ref,
                    qa_ref, ka_ref, va_ref, qb_ref, kb_ref, vb_ref, ga_ref, gb_ref):
    half = ROPE_DIM // 2
    rest = HEAD_DIM // half - 2
    scale = HEAD_DIM ** -0.5
    sigmoid = lambda t: 1.0 / (1.0 + jnp.exp(-t))
    w, d = MIX_WIDTH, ga_ref.shape[1]

    for r in range(0, x_ref.shape[0], PROJ_GROUP):
        rows = slice(r, r + PROJ_GROUP)
        n = _rms(x_ref[rows, :], g_ref[...]).astype(BF16)

        ang = freq_ref[...] * pos_ref[0, :, rows]
        cos, sin = jnp.cos(ang), jnp.sin(ang)
        one, zero = jnp.ones_like(ang), jnp.zeros_like(ang)

        def table(x1, x2, other):
            return jnp.concatenate(([x1, x2] + [other] * rest) * (LANES // HEAD_DIM), axis=0).T

        cos_t = table(cos, cos, one)
        sin_lo = table(-sin, zero, zero)
        sin_hi = table(zero, sin, zero)

        def rope(t):
            tiles = []
            for c in range(t.shape[1] // LANES):
                tc = t[:, c * LANES:(c + 1) * LANES]
                tiles.append(tc * cos_t + pltpu.roll(tc, LANES - half, 1) * sin_lo
                             + pltpu.roll(tc, half, 1) * sin_hi)
            return jnp.concatenate(tiles, axis=1)

        outputs = ((qa_ref, w, lambda t: rope(t) * (scale * LOG2_E)), (ka_ref, w, rope), (va_ref, w, None),
                   (qb_ref, w, lambda t: t * scale), (kb_ref, w, None), (vb_ref, w, None),
                   (ga_ref, d, sigmoid), (gb_ref, d, sigmoid))
        col = 0
        for out_ref, width, post in outputs:
            for c in range(0, width, PROJ_CHUNK):
                t = jnp.dot(n, w_ref[:, col + c:col + c + PROJ_CHUNK], preferred_element_type=F32)
                out_ref[rows, c:c + PROJ_CHUNK] = (t if post is None else post(t)).astype(BF16)
            col += width


def _in_proj(x2d, pos_rows, g, w_in, freq):
    rows, d = x2d.shape
    tm = PROJ_ROWS
    row_block = lambda width: pl.BlockSpec((tm, width), lambda i: (i, 0))
    pos_spec = pl.BlockSpec((1, 1, tm), lambda i: (i, 0, 0))
    out_shape = [jax.ShapeDtypeStruct((rows, MIX_WIDTH), BF16)] * 6 + [jax.ShapeDtypeStruct((rows, d), BF16)] * 2
    return pl.pallas_call(
        _in_proj_kernel,
        out_shape=out_shape,
        grid=(rows // tm,),
        in_specs=[row_block(d), pos_spec, _resident((1, d)), _resident(w_in.shape), _resident(freq.shape)],
        out_specs=[row_block(MIX_WIDTH)] * 6 + [row_block(d)] * 2,
        compiler_params=_params(
            ("parallel",),
            pipelined=(_nbytes((tm, d), F32) + _nbytes((8, tm), F32)
                       + 6 * _nbytes((tm, MIX_WIDTH), BF16) + 2 * _nbytes((tm, d), BF16)),
            resident=_nbytes(w_in.shape, BF16) + 2 * _nbytes((8, d), F32),
            live=2 * (_nbytes((PROJ_GROUP, d), F32) + _nbytes((PROJ_GROUP, d), BF16)
                      + 3 * _nbytes((PROJ_GROUP, LANES), F32) + 4 * _nbytes((PROJ_GROUP, PROJ_CHUNK), F32))),
        name="in_proj",
    )(x2d, pos_rows, g, w_in, freq)


def _scores(qh, kj):
    return lax.dot_general(qh, kj, (((1,), (1,)), ((), ())), preferred_element_type=F32)


def _interleave(*streams):
    order = []
    for s, stream in enumerate(streams):
        total, done = sum(w for w, _ in stream), 0.0
        for k, (w, _) in enumerate(stream):
            order.append(((done + 0.5 * w) / total, s, k))
            done += w
    for _, s, k in sorted(order):
        streams[s][k][1]()


def _split_heads(q):
    first = lax.broadcasted_iota(jnp.int32, (1, LANES), 1) < HEAD_DIM
    zero = jnp.zeros_like(q)
    return jnp.where(first, q, zero), jnp.where(first, zero, q)


def _mixers_kernel(qa_ref, ka_ref, va_ref, qb_ref, kb_ref, vb_ref, bias_ref, tri_ref, mask_ref,
                   oa_ref, ob_ref, vones_ref, s_ref, c_ref, acc_ref, sp_ref, ls_ref, a_ref):
    blk = ATTN_BLOCK
    nq = ka_ref.shape[0] // blk
    first = lax.broadcasted_iota(jnp.int32, (1, LANES), 1) < HEAD_DIM
    block = lambda ref, i: ref[i * blk:(i + 1) * blk, :]

    tile = lambda i, j: i * (i + 1) // 2 + j
    vones_ref[:, :LANES] = va_ref[...]
    vones_ref[:, LANES:] = jnp.ones((ka_ref.shape[0], LANES), BF16)
    row_max, a_out = {}, {}

    def a_scores(i, j, h):
        s = _scores(_split_heads(block(qa_ref, i))[h], block(ka_ref, j)) + bias_ref[i - j]
        s_ref[h, tile(i, j)] = s
        part = jnp.maximum(s[:, :LANES], s[:, LANES:])
        row_max[i, h] = part if j == 0 else jnp.maximum(row_max[i, h], part)

    def a_output(i, h):
        m = jnp.broadcast_to(jnp.max(row_max[i, h], axis=-1, keepdims=True), (blk, LANES))
        m = jnp.concatenate([m] * (blk // LANES), axis=1)
        acc = None
        for j in range(i + 1):
            p = jnp.exp2(s_ref[h, tile(i, j)] - m).astype(BF16)
            pv = jnp.dot(p, block(vones_ref, j), preferred_element_type=F32)
            acc = pv if acc is None else acc + pv
        a_out[i, h] = acc[:, :LANES] / acc[:, LANES:]
        if h == 1:
            oa_ref[i * blk:(i + 1) * blk, :] = jnp.where(first, a_out[i, 0], a_out[i, 1]).astype(BF16)

    a_rows = []
    for i in range(nq + 1):
        row = [(1, functools.partial(a_scores, i, j, h)) for j in range(i + 1) for h in range(2)] if i < nq else []
        if i > 0:
            row += [(i, functools.partial(a_output, i - 1, h)) for h in range(2)]
        a_rows.append(row)
    a_first = sum(a_rows[:A_SPLIT], [])
    a_second = sum(a_rows[A_SPLIT:], [])

    tri = tri_ref[...]
    half = blk // 2
    full_tile = ((slice(0, blk), slice(0, blk)),)
    diagonal_tile = ((slice(0, half), slice(0, half)), (slice(half, blk), slice(0, blk)))
    near_head, near_tail = ((slice(0, B_NEAR_ROWS), slice(half, blk)),), ((slice(B_NEAR_ROWS, blk), slice(half, blk)),)
    far_head, far_tail = ((slice(0, B_FAR_ROWS), slice(0, half)),), ((slice(B_FAR_ROWS, blk), slice(0, half)),)

    def b_sweep(blocks, key_start, d, parts):
        first_col, total = {}, {}

        def softplus_pass(i, h, part):
            rows, keys = part
            qh = _split_heads(block(qb_ref, i))[h]
            z = _scores(qh[rows], kb_ref[pl.ds(key_start(i), blk), :][keys])
            if d is None:
                z = z + mask_ref[rows, keys]
            sp = jnp.maximum(z, 0.0) + jnp.log(1.0 + jnp.exp2(jnp.abs(z) * -LOG2_E))
            ls_ref[i, h, rows, keys] = z - sp
            sp_ref[i, h, rows, keys] = sp.astype(BF16)
            first_col[i, h, rows.start, keys.start] = sp[:, :1]

        def weight_pass(i, h, part):
            rows, keys = part
            width = keys.stop - keys.start
            within = jnp.dot(sp_ref[i, h, rows, keys], tri[:width, :width], preferred_element_type=F32)
            logit = ls_ref[i, h, rows, keys] - within
            if d is not None:
                logit = logit - jnp.concatenate([c_ref[i, h, rows, :]] * (width // LANES), axis=1)
            a_ref[i, h, rows, keys] = jnp.exp(logit).astype(BF16)
            total[i, h, rows.start, keys.start] = jnp.broadcast_to(
                within[:, :1] + first_col[i, h, rows.start, keys.start], (within.shape[0], LANES))

        def value_pass(i, h, part):
            rows, keys = part
            vj = vb_ref[pl.ds(key_start(i), blk), :][keys]
            pv = jnp.dot(a_ref[i, h, rows, keys], vj, preferred_element_type=F32)
            if d is None:
                c_ref[i, h, rows, :] = total[i, h, rows.start, keys.start]
                acc_ref[i, h, rows, :] = pv
                return
            live = 1.0 if isinstance(d, int) else (i >= d).astype(F32)
            c_ref[i, h, rows, :] += live * total[i, h, rows.start, keys.start]
            acc_ref[i, h, rows, :] += live * pv

        tiles = [(i, h, part) for i in blocks for h in range(2) for part in parts]
        area = lambda part: (part[0].stop - part[0].start) * (part[1].stop - part[1].start)
        return [(area(part), functools.partial(fn, i, h, part))
                for fn in (softplus_pass, weight_pass, value_pass) for i, h, part in tiles]

    def min_mass(blocks, rows, unfinished=None):
        least = None
        for i in blocks:
            for h in range(2):
                c = c_ref[i, h, rows, :]
                if unfinished is not None:
                    c = c + (1.0 - unfinished(i)) * (2.0 * SB_DEAD_MASS)
                least = c if least is None else jnp.minimum(least, c)
        return jnp.min(least)

    _interleave(a_first, b_sweep(range(nq), lambda i: i * blk, None, diagonal_tile))
    previous = lambda i: (i - 1) * blk
    one_back = lambda parts: b_sweep(range(1, nq), previous, 1, parts)
    _interleave(a_second, one_back(near_head) + one_back(far_head))
    for parts in (near_tail, far_tail):
        @pl.when(min_mass(range(1, nq), parts[0][0]) <= SB_DEAD_MASS)
        def _():
            _interleave(one_back(parts))

    def body(state):
        d, _ = state
        start = lambda i: pl.multiple_of(jnp.maximum(i - d, 0) * blk, blk)
        _interleave(b_sweep(range(2, nq), start, d, full_tile))
        alive = min_mass(range(2, nq), slice(0, blk), lambda i: (i > d).astype(F32)) <= SB_DEAD_MASS
        return d + 1, (d + 1 < nq) & alive

    if nq > 2:
        lax.while_loop(lambda state: state[1], body,
                       (jnp.int32(2), min_mass(range(2, nq), slice(0, blk)) <= SB_DEAD_MASS))
    for i in range(nq):
        ob_ref[i * blk:(i + 1) * blk, :] = jnp.where(first, acc_ref[i, 0], acc_ref[i, 1]).astype(BF16)


def _mixers(qa, ka, va, qb, kb, vb, bias, tri, strict, batch, seq):
    nq = seq // ATTN_BLOCK
    spec = pl.BlockSpec((seq, LANES), lambda b, p: (b, p))
    state = pltpu.VMEM((nq, 2, ATTN_BLOCK, LANES), F32)
    tiles = lambda dtype: pltpu.VMEM((nq, 2, ATTN_BLOCK, ATTN_BLOCK), dtype)
    scratch = [pltpu.VMEM((seq, 2 * LANES), BF16),
               pltpu.VMEM((2, nq * (nq + 1) // 2, ATTN_BLOCK, ATTN_BLOCK), F32),
               state, state, tiles(BF16), tiles(F32), tiles(BF16)]
    out = jax.ShapeDtypeStruct(qa.shape, BF16)
    return pl.pallas_call(
        _mixers_kernel,
        out_shape=[out, out],
        grid=(batch, MIX_WIDTH // LANES),
        in_specs=[spec] * 6 + [_resident(bias.shape), _resident(tri.shape), _resident(strict.shape)],
        out_specs=[spec, spec],
        scratch_shapes=scratch,
        compiler_params=_params(
            ("parallel", "parallel"),
            pipelined=8 * _nbytes((seq, LANES), BF16),
            resident=_nbytes(bias.shape, F32) + _nbytes(tri.shape, BF16) + _nbytes(strict.shape, F32),
            scratch=sum(_nbytes(buf.shape, buf.dtype) for buf in scratch),
            live=MIXER_LIVE_TILES * _nbytes((ATTN_BLOCK, ATTN_BLOCK), F32)),
        name="mixers",
    )(qa, ka, va, qb, kb, vb, bias, tri, strict)


def _dilated_bias(seq):
    blk = ATTN_BLOCK
    kd = np.arange(seq // blk)[:, None, None]
    dist = kd * blk + np.arange(blk)[None, :, None] - np.arange(blk)[None, None, :]
    count = np.zeros(dist.shape, np.float64)
    for window, dilation in DIL_PATTERNS:
        count += (dist >= 0) & (dist <= window) & (dist % dilation == 0)
    return np.where(count > 0, np.log2(np.maximum(count, 1.0)), MASKED).astype(np.float32)


def _sb_constants():
    blk = ATTN_BLOCK
    tri = np.arange(blk)[:, None] > np.arange(blk)[None, :]
    strict = np.arange(blk)[None, :] < np.arange(blk)[:, None]
    return jnp.asarray(tri, BF16), jnp.asarray(np.where(strict, 0.0, MASKED), F32)


def _mix_mem_kernel(x_ref, oa_ref, ob_ref, ga_ref, gb_ref, wua_ref, wub_ref, wout_ref,
                    gq_ref, wq_ref, kv_ref, wo_ref, h_ref):
    for r in range(0, x_ref.shape[0], MIX_GROUP):
        rows = slice(r, r + MIX_GROUP)
        ua = jnp.dot(oa_ref[rows, :], wua_ref[...], preferred_element_type=F32)
        ub = jnp.dot(ob_ref[rows, :], wub_ref[...], preferred_element_type=F32)
        mixed = ga_ref[rows, :].astype(F32) * ua + gb_ref[rows, :].astype(F32) * ub
        h = x_ref[rows, :] + jnp.dot(mixed.astype(BF16), wout_ref[...], preferred_element_type=F32)

        hn = _rms(h, gq_ref[...]).astype(BF16)
        q = jnp.dot(hn, wq_ref[...], preferred_element_type=F32).astype(BF16)
        heads = []
        for hd in range(N_HEADS_MEM):
            lo, hi = hd * MEM_HEAD_DIM, (hd + 1) * MEM_HEAD_DIM
            s = _scores(q[:, lo:hi], kv_ref[:, lo:hi]) * (MEM_HEAD_DIM ** -0.5)
            p = jnp.exp(s - jnp.max(s, axis=-1, keepdims=True))
            o = jnp.dot(p.astype(BF16), kv_ref[:, MEM_WIDTH + lo:MEM_WIDTH + hi], preferred_element_type=F32)
            heads.append(o / jnp.sum(p, axis=-1, keepdims=True))
        o = jnp.concatenate(heads, axis=1).astype(BF16)
        h_ref[rows, :] = h + jnp.dot(o, wo_ref[...], preferred_element_type=F32)


def _mix_mem(x2d, oa, ob, ga, gb, w_up_a, w_up_b, w_out, g_q, w_q, kv, w_o, seq, n_mem):
    rows, d = x2d.shape
    tm = MIX_ROWS
    per_seq = seq // tm
    row_block = lambda width: pl.BlockSpec((tm, width), lambda i: (i, 0))
    kv_spec = pl.BlockSpec((n_mem, kv.shape[1]), lambda i: (i // per_seq, 0))
    return pl.pallas_call(
        _mix_mem_kernel,
        out_shape=jax.ShapeDtypeStruct((rows, d), F32),
        grid=(rows // tm,),
        in_specs=[row_block(d), row_block(MIX_WIDTH), row_block(MIX_WIDTH), row_block(d), row_block(d),
                  _resident(w_up_a.shape), _resident(w_up_b.shape), _resident(w_out.shape),
                  _resident((1, d)), _resident(w_q.shape), kv_spec, _resident(w_o.shape)],
        out_specs=row_block(d),
        compiler_params=_params(
            ("parallel",),
            pipelined=(2 * _nbytes((tm, d), F32) + 2 * _nbytes((tm, MIX_WIDTH), BF16)
                       + 2 * _nbytes((tm, d), BF16) + _nbytes((n_mem, kv.shape[1]), BF16)),
            resident=sum(_nbytes(w.shape, BF16) for w in (w_up_a, w_up_b, w_out, w_q, w_o)) + _nbytes((8, d), F32),
            live=2 * 4 * _nbytes((MIX_GROUP, d), F32)),
        name="mix_mem",
    )(x2d, oa, ob, ga, gb, w_up_a, w_up_b, w_out, g_q, w_q, kv, w_o)


def _ffn_kernel(h_ref, g_ref, wg_ref, wu_ref, wd_ref, gf_ref, o_ref, *, final_norm):
    for r in range(0, h_ref.shape[0], FFN_GROUP):
        rows = slice(r, r + FFN_GROUP)
        h = h_ref[rows, :]
        n = _rms(h, g_ref[...]).astype(BF16)
        acc = jnp.zeros(h.shape, F32)
        for c in range(0, wg_ref.shape[1], FFN_CHUNK):
            gate = jnp.dot(n, wg_ref[:, c:c + FFN_CHUNK], preferred_element_type=F32)
            up = jnp.dot(n, wu_ref[:, c:c + FFN_CHUNK], preferred_element_type=F32)
            act = (gate / (1.0 + jnp.exp(-gate)) * up).astype(BF16)
            acc = acc + jnp.dot(act, wd_ref[c:c + FFN_CHUNK, :], preferred_element_type=F32)
        h = h + acc
        o_ref[rows, :] = _rms(h, gf_ref[...]) if final_norm else h


def _ffn(h2d, g, w_gate, w_up, w_down, g_final, final_norm):
    rows, d = h2d.shape
    tm = FFN_ROWS
    row_block = pl.BlockSpec((tm, d), lambda i: (i, 0))
    return pl.pallas_call(
        functools.partial(_ffn_kernel, final_norm=final_norm),
        out_shape=jax.ShapeDtypeStruct((rows, d), F32),
        grid=(rows // tm,),
        in_specs=[row_block, _resident((1, d)), _resident(w_gate.shape), _resident(w_up.shape),
                  _resident(w_down.shape), _resident((1, d))],
        out_specs=row_block,
        compiler_params=_params(
            ("parallel",),
            pipelined=2 * _nbytes((tm, d), F32),
            resident=sum(_nbytes(w.shape, BF16) for w in (w_gate, w_up, w_down)) + 2 * _nbytes((8, d), F32),
            live=2 * (2 * _nbytes((FFN_GROUP, d), F32) + _nbytes((FFN_GROUP, d), BF16)
                      + 3 * _nbytes((FFN_GROUP, FFN_CHUNK), F32))),
        name="ffn",
    )(h2d, g, w_gate, w_up, w_down, g_final)


def kernel(x, mem, positions, g_mix, w_in, w_up_a, w_up_b, w_out, g_mem_q, g_mem_kv,
           w_q_mem, w_kv_mem, w_o_mem, g_ffn, w_ffn_gate, w_ffn_up, w_ffn_down, g_final):
    batch, seq, d = x.shape
    n_mem = mem.shape[1]
    depth = w_in.shape[0]
    d_ff = w_ffn_gate.shape[-1]
    assert seq % ATTN_BLOCK == 0 and seq % MIX_ROWS == 0 and d_ff % FFN_CHUNK == 0
    assert w_in.shape[-1] == 6 * MIX_WIDTH + 2 * d and w_kv_mem.shape[-1] == 2 * MEM_WIDTH

    row = lambda g: g.reshape(1, d)
    h = x.reshape(batch * seq, d)
    pos_rows = positions.astype(F32).reshape(batch * seq // PROJ_ROWS, 1, PROJ_ROWS)
    mem2d = mem.reshape(batch * n_mem, d)

    half = ROPE_DIM // 2
    freq = (ROPE_THETA ** (-jnp.arange(half, dtype=F32) / half)).reshape(half, 1)
    bias = jnp.asarray(_dilated_bias(seq))
    tri, strict = _sb_constants()

    for l in range(depth):
        qa, ka, va, qb, kb, vb, ga, gb = _in_proj(h, pos_rows, row(g_mix[l]), w_in[l].astype(BF16), freq)
        oa, ob = _mixers(qa, ka, va, qb, kb, vb, bias, tri, strict, batch, seq)
        kv = _mem_kv(mem2d, row(g_mem_kv[l]), w_kv_mem[l].astype(BF16))
        h = _mix_mem(h, oa, ob, ga, gb, w_up_a[l].astype(BF16), w_up_b[l].astype(BF16), w_out[l].astype(BF16),
                     row(g_mem_q[l]), w_q_mem[l].astype(BF16), kv, w_o_mem[l].astype(BF16), seq, n_mem)
        w_gate, w_up, w_down = (w[l].astype(BF16) for w in (w_ffn_gate, w_ffn_up, w_ffn_down))
        h = _ffn(h, row(g_ffn[l]), w_gate, w_up, w_down, row(g_final), final_norm=(l == depth - 1))
    return h.reshape(batch, seq, d)
```

```python
import functools

import numpy as np
import jax
import jax.numpy as jnp
from jax import lax
from jax.experimental import pallas as pl
from jax.experimental.pallas import tpu as pltpu

F32 = jnp.float32
BF16 = jnp.bfloat16

HEAD_DIM = 64
MIX_WIDTH = 512
DIL_PATTERNS = ((128, 1), (512, 4), (2048, 16))
ROPE_THETA = 500000.0
ROPE_DIM = HEAD_DIM // 4
N_HEADS_MEM = 4
MEM_HEAD_DIM = 128
MEM_WIDTH = N_HEADS_MEM * MEM_HEAD_DIM
RMS_EPS = 1e-6

LANES = 128
MIB = 1024 * 1024
V7X_VMEM_BYTES = 64 * MIB
MASKED = -1e30
LOG2_E = 1.4426950408889634
SB_DEAD_MASS = 110.0

ATTN_BLOCK = 256
A_SPLIT = 6
B_TAIL_ROWS = 64
MIXER_LIVE_TILES = 56
PROJ_ROWS = 1024
PROJ_GROUP = 256
PROJ_CHUNK = 256
MIX_ROWS = 1024
MIX_GROUP = 512
FFN_ROWS = 1024
FFN_GROUP = 512
FFN_CHUNK = 256


def _rms(x, g):
    y = x * lax.rsqrt(jnp.mean(x * x, axis=-1, keepdims=True) + RMS_EPS)
    return y * g


def _resident(shape):
    zeros = (0,) * len(shape)
    return pl.BlockSpec(shape, lambda *_: zeros, pipeline_mode=pl.Buffered(1))


def _nbytes(shape, dtype):
    return int(np.prod(shape)) * jnp.dtype(dtype).itemsize


def _params(semantics, pipelined, resident, scratch=0, live=0):
    need = 2 * pipelined + resident + scratch + live
    assert need <= V7X_VMEM_BYTES, need
    return pltpu.CompilerParams(dimension_semantics=semantics, vmem_limit_bytes=need)


def _in_proj_kernel(x_ref, pos_ref, g_ref, w_ref, freq_ref, mem_ref, gkv_ref, wkv_ref,
                    qa_ref, ka_ref, va_ref, qb_ref, kb_ref, vb_ref, ga_ref, gb_ref, kv_ref):
    half = ROPE_DIM // 2
    rest = HEAD_DIM // half - 2
    scale = HEAD_DIM ** -0.5
    sigmoid = lambda t: 1.0 / (1.0 + jnp.exp(-t))
    w, d = MIX_WIDTH, ga_ref.shape[1]

    for r in range(0, x_ref.shape[0], PROJ_GROUP):
        rows = slice(r, r + PROJ_GROUP)
        n = _rms(x_ref[rows, :], g_ref[...]).astype(BF16)

        ang = freq_ref[...] * pos_ref[0, :, rows]
        cos, sin = jnp.cos(ang), jnp.sin(ang)
        one, zero = jnp.ones_like(ang), jnp.zeros_like(ang)

        def table(x1, x2, other):
            return jnp.concatenate(([x1, x2] + [other] * rest) * (LANES // HEAD_DIM), axis=0).T

        cos_t = table(cos, cos, one)
        sin_lo = table(-sin, zero, zero)
        sin_hi = table(zero, sin, zero)

        def rope(t):
            tiles = []
            for c in range(t.shape[1] // LANES):
                tc = t[:, c * LANES:(c + 1) * LANES]
                tiles.append(tc * cos_t + pltpu.roll(tc, LANES - half, 1) * sin_lo
                             + pltpu.roll(tc, half, 1) * sin_hi)
            return jnp.concatenate(tiles, axis=1)

        outputs = ((qa_ref, w, lambda t: rope(t) * (scale * LOG2_E)), (ka_ref, w, rope), (va_ref, w, None),
                   (qb_ref, w, lambda t: t * scale), (kb_ref, w, None), (vb_ref, w, None),
                   (ga_ref, d, sigmoid), (gb_ref, d, sigmoid))
        col = 0
        for out_ref, width, post in outputs:
            for c in range(0, width, PROJ_CHUNK):
                t = jnp.dot(n, w_ref[:, col + c:col + c + PROJ_CHUNK], preferred_element_type=F32)
                out_ref[rows, c:c + PROJ_CHUNK] = (t if post is None else post(t)).astype(BF16)
            col += width

    mem_n = _rms(mem_ref[...], gkv_ref[...]).astype(BF16)
    kv_ref[...] = jnp.dot(mem_n, wkv_ref[...], preferred_element_type=F32).astype(BF16)


def _in_proj(x2d, pos_rows, g, w_in, freq, mem2d, g_kv, w_kv):
    rows, d = x2d.shape
    tm = PROJ_ROWS
    mem_rows = mem2d.shape[0] * tm // rows
    assert mem_rows * (rows // tm) == mem2d.shape[0] and mem_rows % 16 == 0
    mem_block = lambda width: pl.BlockSpec((mem_rows, width), lambda i: (i, 0))
    row_block = lambda width: pl.BlockSpec((tm, width), lambda i: (i, 0))
    pos_spec = pl.BlockSpec((1, 1, tm), lambda i: (i, 0, 0))
    out_shape = ([jax.ShapeDtypeStruct((rows, MIX_WIDTH), BF16)] * 6 + [jax.ShapeDtypeStruct((rows, d), BF16)] * 2
                 + [jax.ShapeDtypeStruct((mem2d.shape[0], w_kv.shape[1]), BF16)])
    return pl.pallas_call(
        _in_proj_kernel,
        out_shape=out_shape,
        grid=(rows // tm,),
        in_specs=[row_block(d), pos_spec, _resident((1, d)), _resident(w_in.shape), _resident(freq.shape),
                  mem_block(d), _resident((1, d)), _resident(w_kv.shape)],
        out_specs=[row_block(MIX_WIDTH)] * 6 + [row_block(d)] * 2 + [mem_block(w_kv.shape[1])],
        compiler_params=_params(
            ("parallel",),
            pipelined=(_nbytes((tm, d), F32) + _nbytes((8, tm), F32)
                       + 6 * _nbytes((tm, MIX_WIDTH), BF16) + 2 * _nbytes((tm, d), BF16)
                       + _nbytes((mem_rows, d), F32) + _nbytes((mem_rows, w_kv.shape[1]), BF16)),
            resident=_nbytes(w_in.shape, BF16) + _nbytes(w_kv.shape, BF16) + 3 * _nbytes((8, d), F32),
            live=2 * (_nbytes((PROJ_GROUP, d), F32) + _nbytes((PROJ_GROUP, d), BF16)
                      + 3 * _nbytes((PROJ_GROUP, LANES), F32) + 4 * _nbytes((PROJ_GROUP, PROJ_CHUNK), F32))),
        name="in_proj",
    )(x2d, pos_rows, g, w_in, freq, mem2d, g_kv, w_kv)


def _scores(qh, kj):
    return lax.dot_general(qh, kj, (((1,), (1,)), ((), ())), preferred_element_type=F32)


def _interleave(*streams):
    order = []
    for s, stream in enumerate(streams):
        total, done = sum(w for w, _ in stream), 0.0
        for k, (w, _) in enumerate(stream):
            order.append(((done + 0.5 * w) / total, s, k))
            done += w
    for _, s, k in sorted(order):
        streams[s][k][1]()


def _split_heads(q):
    first = lax.broadcasted_iota(jnp.int32, (1, LANES), 1) < HEAD_DIM
    zero = jnp.zeros_like(q)
    return jnp.where(first, q, zero), jnp.where(first, zero, q)


def _mixers_kernel(qa_ref, ka_ref, va_ref, qb_ref, kb_ref, vb_ref, bias_ref, tri_ref, mask_ref,
                   oa_ref, ob_ref, vones_ref, s_ref, c_ref, acc_ref, sp_ref, ls_ref, a_ref):
    blk = ATTN_BLOCK
    nq = ka_ref.shape[0] // blk
    first = lax.broadcasted_iota(jnp.int32, (1, LANES), 1) < HEAD_DIM
    block = lambda ref, i: ref[i * blk:(i + 1) * blk, :]

    tile = lambda i, j: i * (i + 1) // 2 + j
    vones_ref[:, :LANES] = va_ref[...]
    vones_ref[:, LANES:] = jnp.ones((ka_ref.shape[0], LANES), BF16)
    row_max, a_out = {}, {}

    def a_scores(i, j, h):
        s = _scores(_split_heads(block(qa_ref, i))[h], block(ka_ref, j)) + bias_ref[i - j]
        s_ref[h, tile(i, j)] = s
        part = jnp.maximum(s[:, :LANES], s[:, LANES:])
        row_max[i, h] = part if j == 0 else jnp.maximum(row_max[i, h], part)

    def a_output(i, h):
        m = jnp.broadcast_to(jnp.max(row_max[i, h], axis=-1, keepdims=True), (blk, LANES))
        m = jnp.concatenate([m] * (blk // LANES), axis=1)
        acc = None
        for j in range(i + 1):
            p = jnp.exp2(s_ref[h, tile(i, j)] - m).astype(BF16)
            pv = jnp.dot(p, block(vones_ref, j), preferred_element_type=F32)
            acc = pv if acc is None else acc + pv
        a_out[i, h] = acc[:, :LANES] / acc[:, LANES:]
        if h == 1:
            oa_ref[i * blk:(i + 1) * blk, :] = jnp.where(first, a_out[i, 0], a_out[i, 1]).astype(BF16)

    a_rows = []
    for i in range(nq + 1):
        row = [(1, functools.partial(a_scores, i, j, h)) for j in range(i + 1) for h in range(2)] if i < nq else []
        if i > 0:
            row += [(i, functools.partial(a_output, i - 1, h)) for h in range(2)]
        a_rows.append(row)
    a_first = sum(a_rows[:A_SPLIT], [])
    a_second = sum(a_rows[A_SPLIT:], [])

    tri = tri_ref[...]
    full_tile = ((slice(0, blk), blk),)
    diagonal_tile = ((slice(0, blk // 2), blk // 2), (slice(blk // 2, blk), blk))
    head_rows, tail_rows = ((slice(0, blk - B_TAIL_ROWS), blk),), ((slice(blk - B_TAIL_ROWS, blk), blk),)

    def b_sweep(blocks, key_start, d, parts):
        first_col, total = {}, {}

        def softplus_pass(i, h, part):
            rows, keys = part
            qh = _split_heads(block(qb_ref, i))[h]
            z = _scores(qh[rows], kb_ref[pl.ds(key_start(i), blk), :][:keys])
            if d is None:
                z = z + mask_ref[rows, :keys]
            sp = jnp.maximum(z, 0.0) + jnp.log(1.0 + jnp.exp2(jnp.abs(z) * -LOG2_E))
            ls_ref[i, h, rows, :keys] = z - sp
            sp_ref[i, h, rows, :keys] = sp.astype(BF16)
            first_col[i, h, rows.start] = sp[:, :1]

        def weight_pass(i, h, part):
            rows, keys = part
            within = jnp.dot(sp_ref[i, h, rows, :keys], tri[:keys, :keys], preferred_element_type=F32)
            logit = ls_ref[i, h, rows, :keys] - within
            if d is not None:
                logit = logit - jnp.concatenate([c_ref[i, h, rows, :]] * (keys // LANES), axis=1)
            a_ref[i, h, rows, :keys] = jnp.exp(logit).astype(BF16)
            total[i, h, rows.start] = jnp.broadcast_to(within[:, :1] + first_col[i, h, rows.start],
                                                       (within.shape[0], LANES))

        def value_pass(i, h, part):
            rows, keys = part
            vj = vb_ref[pl.ds(key_start(i), blk), :][:keys]
            pv = jnp.dot(a_ref[i, h, rows, :keys], vj, preferred_element_type=F32)
            if d is None:
                c_ref[i, h, rows, :] = total[i, h, rows.start]
                acc_ref[i, h, rows, :] = pv
                return
            live = 1.0 if isinstance(d, int) else (i >= d).astype(F32)
            c_ref[i, h, rows, :] += live * total[i, h, rows.start]
            acc_ref[i, h, rows, :] += live * pv

        tiles = [(i, h, part) for i in blocks for h in range(2) for part in parts]
        return [(part[1], functools.partial(fn, i, h, part))
                for fn in (softplus_pass, weight_pass, value_pass) for i, h, part in tiles]

    def min_mass(blocks, rows, unfinished=None):
        least = None
        for i in blocks:
            for h in range(2):
                c = c_ref[i, h, rows, :]
                if unfinished is not None:
                    c = c + (1.0 - unfinished(i)) * (2.0 * SB_DEAD_MASS)
                least = c if least is None else jnp.minimum(least, c)
        return jnp.min(least)

    _interleave(a_first, b_sweep(range(nq), lambda i: i * blk, None, diagonal_tile))
    previous = lambda i: (i - 1) * blk
    _interleave(a_second, b_sweep(range(1, nq), previous, 1, head_rows))

    @pl.when(min_mass(range(1, nq), tail_rows[0][0]) <= SB_DEAD_MASS)
    def _():
        _interleave(b_sweep(range(1, nq), previous, 1, tail_rows))

    def body(state):
        d, _ = state
        start = lambda i: pl.multiple_of(jnp.maximum(i - d, 0) * blk, blk)
        _interleave(b_sweep(range(2, nq), start, d, full_tile))
        alive = min_mass(range(2, nq), slice(0, blk), lambda i: (i > d).astype(F32)) <= SB_DEAD_MASS
        return d + 1, (d + 1 < nq) & alive

    if nq > 2:
        lax.while_loop(lambda state: state[1], body,
                       (jnp.int32(2), min_mass(range(2, nq), slice(0, blk)) <= SB_DEAD_MASS))
    for i in range(nq):
        ob_ref[i * blk:(i + 1) * blk, :] = jnp.where(first, acc_ref[i, 0], acc_ref[i, 1]).astype(BF16)


def _mixers(qa, ka, va, qb, kb, vb, bias, tri, strict, batch, seq):
    nq = seq // ATTN_BLOCK
    spec = pl.BlockSpec((seq, LANES), lambda b, p: (b, p))
    state = pltpu.VMEM((nq, 2, ATTN_BLOCK, LANES), F32)
    tiles = lambda dtype: pltpu.VMEM((nq, 2, ATTN_BLOCK, ATTN_BLOCK), dtype)
    scratch = [pltpu.VMEM((seq, 2 * LANES), BF16),
               pltpu.VMEM((2, nq * (nq + 1) // 2, ATTN_BLOCK, ATTN_BLOCK), F32),
               state, state, tiles(BF16), tiles(F32), tiles(BF16)]
    out = jax.ShapeDtypeStruct(qa.shape, BF16)
    return pl.pallas_call(
        _mixers_kernel,
        out_shape=[out, out],
        grid=(batch, MIX_WIDTH // LANES),
        in_specs=[spec] * 6 + [_resident(bias.shape), _resident(tri.shape), _resident(strict.shape)],
        out_specs=[spec, spec],
        scratch_shapes=scratch,
        compiler_params=_params(
            ("parallel", "parallel"),
            pipelined=8 * _nbytes((seq, LANES), BF16),
            resident=_nbytes(bias.shape, F32) + _nbytes(tri.shape, BF16) + _nbytes(strict.shape, F32),
            scratch=sum(_nbytes(buf.shape, buf.dtype) for buf in scratch),
            live=MIXER_LIVE_TILES * _nbytes((ATTN_BLOCK, ATTN_BLOCK), F32)),
        name="mixers",
    )(qa, ka, va, qb, kb, vb, bias, tri, strict)


def _dilated_bias(seq):
    blk = ATTN_BLOCK
    kd = np.arange(seq // blk)[:, None, None]
    dist = kd * blk + np.arange(blk)[None, :, None] - np.arange(blk)[None, None, :]
    count = np.zeros(dist.shape, np.float64)
    for window, dilation in DIL_PATTERNS:
        count += (dist >= 0) & (dist <= window) & (dist % dilation == 0)
    return np.where(count > 0, np.log2(np.maximum(count, 1.0)), MASKED).astype(np.float32)


def _sb_constants():
    blk = ATTN_BLOCK
    tri = np.arange(blk)[:, None] > np.arange(blk)[None, :]
    strict = np.arange(blk)[None, :] < np.arange(blk)[:, None]
    return jnp.asarray(tri, BF16), jnp.asarray(np.where(strict, 0.0, MASKED), F32)


def _mix_mem_kernel(x_ref, oa_ref, ob_ref, ga_ref, gb_ref, wua_ref, wub_ref, wout_ref,
                    gq_ref, wq_ref, kv_ref, wo_ref, h_ref):
    for r in range(0, x_ref.shape[0], MIX_GROUP):
        rows = slice(r, r + MIX_GROUP)
        ua = jnp.dot(oa_ref[rows, :], wua_ref[...], preferred_element_type=F32)
        ub = jnp.dot(ob_ref[rows, :], wub_ref[...], preferred_element_type=F32)
        mixed = ga_ref[rows, :].astype(F32) * ua + gb_ref[rows, :].astype(F32) * ub
        h = x_ref[rows, :] + jnp.dot(mixed.astype(BF16), wout_ref[...], preferred_element_type=F32)

        hn = _rms(h, gq_ref[...]).astype(BF16)
        q = jnp.dot(hn, wq_ref[...], preferred_element_type=F32).astype(BF16)
        heads = []
        for hd in range(N_HEADS_MEM):
            lo, hi = hd * MEM_HEAD_DIM, (hd + 1) * MEM_HEAD_DIM
            s = _scores(q[:, lo:hi], kv_ref[:, lo:hi]) * (MEM_HEAD_DIM ** -0.5)
            p = jnp.exp(s - jnp.max(s, axis=-1, keepdims=True))
            o = jnp.dot(p.astype(BF16), kv_ref[:, MEM_WIDTH + lo:MEM_WIDTH + hi], preferred_element_type=F32)
            heads.append(o / jnp.sum(p, axis=-1, keepdims=True))
        o = jnp.concatenate(heads, axis=1).astype(BF16)
        h_ref[rows, :] = h + jnp.dot(o, wo_ref[...], preferred_element_type=F32)


def _mix_mem(x2d, oa, ob, ga, gb, w_up_a, w_up_b, w_out, g_q, w_q, kv, w_o, seq, n_mem):
    rows, d = x2d.shape
    tm = MIX_ROWS
    per_seq = seq // tm
    row_block = lambda width: pl.BlockSpec((tm, width), lambda i: (i, 0))
    kv_spec = pl.BlockSpec((n_mem, kv.shape[1]), lambda i: (i // per_seq, 0))
    return pl.pallas_call(
        _mix_mem_kernel,
        out_shape=jax.ShapeDtypeStruct((rows, d), F32),
        grid=(rows // tm,),
        in_specs=[row_block(d), row_block(MIX_WIDTH), row_block(MIX_WIDTH), row_block(d), row_block(d),
                  _resident(w_up_a.shape), _resident(w_up_b.shape), _resident(w_out.shape),
                  _resident((1, d)), _resident(w_q.shape), kv_spec, _resident(w_o.shape)],
        out_specs=row_block(d),
        compiler_params=_params(
            ("parallel",),
            pipelined=(2 * _nbytes((tm, d), F32) + 2 * _nbytes((tm, MIX_WIDTH), BF16)
                       + 2 * _nbytes((tm, d), BF16) + _nbytes((n_mem, kv.shape[1]), BF16)),
            resident=sum(_nbytes(w.shape, BF16) for w in (w_up_a, w_up_b, w_out, w_q, w_o)) + _nbytes((8, d), F32),
            live=2 * 4 * _nbytes((MIX_GROUP, d), F32)),
        name="mix_mem",
    )(x2d, oa, ob, ga, gb, w_up_a, w_up_b, w_out, g_q, w_q, kv, w_o)


def _ffn_kernel(h_ref, g_ref, wg_ref, wu_ref, wd_ref, gf_ref, o_ref, *, final_norm):
    for r in range(0, h_ref.shape[0], FFN_GROUP):
        rows = slice(r, r + FFN_GROUP)
        h = h_ref[rows, :]
        n = _rms(h, g_ref[...]).astype(BF16)
        acc = jnp.zeros(h.shape, F32)
        for c in range(0, wg_ref.shape[1], FFN_CHUNK):
            gate = jnp.dot(n, wg_ref[:, c:c + FFN_CHUNK], preferred_element_type=F32)
            up = jnp.dot(n, wu_ref[:, c:c + FFN_CHUNK], preferred_element_type=F32)
            act = (gate / (1.0 + jnp.exp(-gate)) * up).astype(BF16)
            acc = acc + jnp.dot(act, wd_ref[c:c + FFN_CHUNK, :], preferred_element_type=F32)
        h = h + acc
        o_ref[rows, :] = _rms(h, gf_ref[...]) if final_norm else h


def _ffn(h2d, g, w_gate, w_up, w_down, g_final, final_norm):
    rows, d = h2d.shape
    tm = FFN_ROWS
    row_block = pl.BlockSpec((tm, d), lambda i: (i, 0))
    return pl.pallas_call(
        functools.partial(_ffn_kernel, final_norm=final_norm),
        out_shape=jax.ShapeDtypeStruct((rows, d), F32),
        grid=(rows // tm,),
        in_specs=[row_block, _resident((1, d)), _resident(w_gate.shape), _resident(w_up.shape),
                  _resident(w_down.shape), _resident((1, d))],
        out_specs=row_block,
        compiler_params=_params(
            ("parallel",),
            pipelined=2 * _nbytes((tm, d), F32),
            resident=sum(_nbytes(w.shape, BF16) for w in (w_gate, w_up, w_down)) + 2 * _nbytes((8, d), F32),
            live=2 * (2 * _nbytes((FFN_GROUP, d), F32) + _nbytes((FFN_GROUP, d), BF16)
                      + 3 * _nbytes((FFN_GROUP, FFN_CHUNK), F32))),
        name="ffn",
    )(h2d, g, w_gate, w_up, w_down, g_final)


def kernel(x, mem, positions, g_mix, w_in, w_up_a, w_up_b, w_out, g_mem_q, g_mem_kv,
           w_q_mem, w_kv_mem, w_o_mem, g_ffn, w_ffn_gate, w_ffn_up, w_ffn_down, g_final):
    batch, seq, d = x.shape
    n_mem = mem.shape[1]
    depth = w_in.shape[0]
    d_ff = w_ffn_gate.shape[-1]
    assert seq % ATTN_BLOCK == 0 and seq % MIX_ROWS == 0 and d_ff % FFN_CHUNK == 0
    assert w_in.shape[-1] == 6 * MIX_WIDTH + 2 * d and w_kv_mem.shape[-1] == 2 * MEM_WIDTH

    row = lambda g: g.reshape(1, d)
    h = x.reshape(batch * seq, d)
    pos_rows = positions.astype(F32).reshape(batch * seq // PROJ_ROWS, 1, PROJ_ROWS)
    mem2d = mem.reshape(batch * n_mem, d)

    half = ROPE_DIM // 2
    freq = (ROPE_THETA ** (-jnp.arange(half, dtype=F32) / half)).reshape(half, 1)
    bias = jnp.asarray(_dilated_bias(seq))
    tri, strict = _sb_constants()

    for l in range(depth):
        qa, ka, va, qb, kb, vb, ga, gb, kv = _in_proj(h, pos_rows, row(g_mix[l]), w_in[l].astype(BF16), freq,
                                                      mem2d, row(g_mem_kv[l]), w_kv_mem[l].astype(BF16))
        oa, ob = _mixers(qa, ka, va, qb, kb, vb, bias, tri, strict, batch, seq)
        h = _mix_mem(h, oa, ob, ga, gb, w_up_a[l].astype(BF16), w_up_b[l].astype(BF16), w_out[l].astype(BF16),
                     row(g_mem_q[l]), w_q_mem[l].astype(BF16), kv, w_o_mem[l].astype(BF16), seq, n_mem)
        w_gate, w_up, w_down = (w[l].astype(BF16) for w in (w_ffn_gate, w_ffn_up, w_ffn_down))
        h = _ffn(h, row(g_ffn[l]), w_gate, w_up, w_down, row(g_final), final_norm=(l == depth - 1))
    return h.reshape(batch, seq, d)
```
